```python
import math
import jax, jax.numpy as jnp
from jax import lax
import numpy as np

D_MODEL = 1024
BATCH = 16
SEQ = 2048
DEPTH = 1
DEC_BATCH = 32
DEC_SEQ = 16
PAST_LEN = 2048

CHUNK = 64
Q_BLOCK = 128
BRANCH_WIDTH = D_MODEL // 2
DA_HEAD_DIM = 64
DA_HEADS = BRANCH_WIDTH // (2 * DA_HEAD_DIM)
DA_V = 2 * DA_HEAD_DIM
SB_HEAD_DIM = 64
SB_HEADS = BRANCH_WIDTH // SB_HEAD_DIM
D_FF = 2816
CONV_WIDTH = 3
ROPE_THETA = 10000.0
NORM_EPS = 1e-6
N_IN = 6 * BRANCH_WIDTH + 2 * D_MODEL
NEG_INF = -1e30

kernel_name = "gated_diffattn_stickbreak_convffn_stream_step"


def rms_norm(x, g):
    xf = x.astype(jnp.float32)
    y = xf * lax.rsqrt(jnp.mean(xf * xf, axis=-1, keepdims=True) + NORM_EPS)
    return (y * g.astype(jnp.float32)).astype(x.dtype)


def rope(x, pos):
    dh = x.shape[-1]
    half = dh // 2
    inv = ROPE_THETA ** (-jnp.arange(half, dtype=jnp.float32) * 2.0 / dh)
    ang = pos.astype(jnp.float32)[:, None] * inv[None, :]
    cos = jnp.cos(ang)[:, None, None, :]
    sin = jnp.sin(ang)[:, None, None, :]
    xf = x.astype(jnp.float32)
    x1, x2 = xf[..., :half], xf[..., half:]
    out = jnp.concatenate([x1 * cos - x2 * sin, x2 * cos + x1 * sin], axis=-1)
    return out.astype(x.dtype)


def split_projection(z, pos):
    lead = z.shape[:2]
    W = BRANCH_WIDTH
    qa = rope(z[..., 0:W].reshape(lead + (DA_HEADS, 2, DA_HEAD_DIM)), pos)
    ka = rope(z[..., W:2 * W].reshape(lead + (DA_HEADS, 2, DA_HEAD_DIM)), pos)
    va = z[..., 2 * W:3 * W].reshape(lead + (DA_HEADS, DA_V))
    qb = z[..., 3 * W:4 * W].reshape(lead + (SB_HEADS, SB_HEAD_DIM))
    kb = z[..., 4 * W:5 * W].reshape(lead + (SB_HEADS, SB_HEAD_DIM))
    vb = z[..., 5 * W:6 * W].reshape(lead + (SB_HEADS, SB_HEAD_DIM))
    gate_a = jax.nn.sigmoid(z[..., 6 * W:6 * W + D_MODEL])
    gate_b = jax.nn.sigmoid(z[..., 6 * W + D_MODEL:])
    return qa, ka, va, qb, kb, vb, gate_a, gate_b


def diff_attention(q, k, v, q_pos, k_pos, lam):
    s = jnp.einsum('bqhcd,bkhcd->bhcqk', q, k).astype(jnp.float32) * (DA_HEAD_DIM ** -0.5)
    mask = (k_pos[None, :] // CHUNK) <= (q_pos[:, None] // CHUNK)
    p = jax.nn.softmax(jnp.where(mask, s, NEG_INF), axis=-1)
    w = p[:, :, 0] - lam * p[:, :, 1]
    return jnp.einsum('bhqk,bkhe->bqhe', w.astype(v.dtype), v)


def stick_breaking(q, k, v, q_pos, k_pos):
    z = jnp.einsum('bqhd,bkhd->bhqk', q, k).astype(jnp.float32) * (SB_HEAD_DIM ** -0.5)
    mask = k_pos[None, :] < q_pos[:, None]
    log_keep = jnp.where(mask, -jax.nn.softplus(z), 0.0)
    later = lax.cumsum(log_keep, axis=3, reverse=True) - log_keep
    a = jnp.where(mask, jnp.exp(jax.nn.log_sigmoid(z) + later), 0.0)
    return jnp.einsum('bhqk,bkhd->bqhd', a.astype(v.dtype), v)


def merge_branches(oa, ob, gate_a, gate_b, subln_g, lam_init, w_br_a, w_br_b, w_out):
    lead = oa.shape[:2]
    oa = rms_norm(oa, subln_g) * (1.0 - lam_init)
    br_a = oa.reshape(lead + (BRANCH_WIDTH,)) @ w_br_a
    br_b = ob.reshape(lead + (BRANCH_WIDTH,)) @ w_br_b
    return (gate_a * br_a + gate_b * br_b) @ w_out


def conv_ffn(x, prev, g, w_up, conv_w, conv_b, w_down):
    T = x.shape[1]
    u = rms_norm(x, g) @ w_up
    full = jnp.concatenate([prev.astype(u.dtype), u], axis=1)
    c = conv_b
    for j in range(CONV_WIDTH):
        c = c + conv_w[j] * full[:, j:j + T]
    gate, val = c[..., :D_FF], c[..., D_FF:]
    y = (jax.nn.silu(gate) * val) @ w_down
    return x + y, full[:, -(CONV_WIDTH - 1):]


def setup_inputs(seed: int = 0) -> dict:
    key = jax.random.key(seed)
    ks = jax.random.split(key, 32)
    f32 = jnp.float32
    nrm = lambda k, s, scale: jax.random.normal(k, s, f32) * scale
    return {
        "x_prompt": nrm(ks[0], (BATCH, SEQ, D_MODEL), 1.0),
        "x_sample": nrm(ks[1], (DEC_BATCH, DEC_SEQ, D_MODEL), 1.0),
        "cache_diff_k": nrm(ks[2], (DEPTH, DEC_BATCH, PAST_LEN, DA_HEADS, 2, DA_HEAD_DIM), 1.0),
        "cache_diff_v": nrm(ks[3], (DEPTH, DEC_BATCH, PAST_LEN, DA_HEADS, DA_V), 1.0),
        "cache_sb_k": nrm(ks[4], (DEPTH, DEC_BATCH, PAST_LEN, SB_HEADS, SB_HEAD_DIM), 1.0),
        "cache_sb_v": nrm(ks[5], (DEPTH, DEC_BATCH, PAST_LEN, SB_HEADS, SB_HEAD_DIM), 1.0),
        "state_conv": nrm(ks[6], (DEPTH, DEC_BATCH, CONV_WIDTH - 1, 2 * D_FF), 1.0),
        "attn_norm_g": 1.0 + nrm(ks[7], (DEPTH, D_MODEL), 0.02),
        "w_in": nrm(ks[8], (DEPTH, D_MODEL, N_IN), D_MODEL ** -0.5),
        "lambda_q1": nrm(ks[9], (DEPTH, DA_HEAD_DIM), 0.1),
        "lambda_k1": nrm(ks[10], (DEPTH, DA_HEAD_DIM), 0.1),
        "lambda_q2": nrm(ks[11], (DEPTH, DA_HEAD_DIM), 0.1),
        "lambda_k2": nrm(ks[12], (DEPTH, DA_HEAD_DIM), 0.1),
        "subln_g": 1.0 + nrm(ks[13], (DEPTH, DA_V), 0.02),
        "w_branch_a": nrm(ks[14], (DEPTH, BRANCH_WIDTH, D_MODEL), BRANCH_WIDTH ** -0.5),
        "w_branch_b": nrm(ks[15], (DEPTH, BRANCH_WIDTH, D_MODEL), BRANCH_WIDTH ** -0.5),
        "w_out": nrm(ks[16], (DEPTH, D_MODEL, D_MODEL), D_MODEL ** -0.5),
        "ffn_norm_g": 1.0 + nrm(ks[17], (DEPTH, D_MODEL), 0.02),
        "w_up": nrm(ks[18], (DEPTH, D_MODEL, 2 * D_FF), D_MODEL ** -0.5),
        "conv_w": nrm(ks[19], (DEPTH, CONV_WIDTH, 2 * D_FF), CONV_WIDTH ** -0.5),
        "conv_b": nrm(ks[20], (DEPTH, 2 * D_FF), 0.01),
        "w_down": nrm(ks[21], (DEPTH, D_FF, D_MODEL), D_FF ** -0.5),
        "final_norm_g": 1.0 + nrm(ks[22], (D_MODEL,), 0.02),
    }


def reference(x_prompt, x_sample, cache_diff_k, cache_diff_v, cache_sb_k, cache_sb_v,
              state_conv, attn_norm_g, w_in, lambda_q1, lambda_k1, lambda_q2, lambda_k2,
              subln_g, w_branch_a, w_branch_b, w_out, ffn_norm_g, w_up, conv_w, conv_b,
              w_down, final_norm_g):
    seq = x_prompt.shape[1]
    t_new = x_sample.shape[1]
    past = cache_diff_k.shape[2]
    pos_p = jnp.arange(seq)
    pos_s = past + jnp.arange(t_new)
    k_pos_s = jnp.arange(past + t_new)

    xp, xs = x_prompt, x_sample
    p_dk, p_dv, p_sk, p_sv, p_cv = [], [], [], [], []
    s_dk, s_dv, s_sk, s_sv, s_cv = [], [], [], [], []
    for l in range(DEPTH):
        lam_init = 0.8 - 0.6 * math.exp(-0.3 * l)
        lam = (jnp.exp(jnp.sum(lambda_q1[l].astype(jnp.float32) * lambda_k1[l].astype(jnp.float32)))
               - jnp.exp(jnp.sum(lambda_q2[l].astype(jnp.float32) * lambda_k2[l].astype(jnp.float32)))
               + lam_init)

        zp = rms_norm(xp, attn_norm_g[l]) @ w_in[l]
        qa, ka, va, qb, kb, vb, ga, gb = split_projection(zp, pos_p)
        oa_blocks, ob_blocks = [], []
        for i in range(seq // Q_BLOCK):
            s0, e0 = i * Q_BLOCK, (i + 1) * Q_BLOCK
            qpos = jnp.arange(s0, e0)
            kpos = jnp.arange(e0)
            oa_blocks.append(diff_attention(qa[:, s0:e0], ka[:, :e0], va[:, :e0], qpos, kpos, lam))
            ob_blocks.append(stick_breaking(qb[:, s0:e0], kb[:, :e0], vb[:, :e0], qpos, kpos))
        oa = jnp.concatenate(oa_blocks, axis=1)
        ob = jnp.concatenate(ob_blocks, axis=1)
        xp = xp + merge_branches(oa, ob, ga, gb, subln_g[l], lam_init,
                                 w_branch_a[l], w_branch_b[l], w_out[l])
        zeros_prev = jnp.zeros((xp.shape[0], CONV_WIDTH - 1, 2 * D_FF), xp.dtype)
        xp, conv_p = conv_ffn(xp, zeros_prev, ffn_norm_g[l], w_up[l], conv_w[l], conv_b[l], w_down[l])
        p_dk.append(ka); p_dv.append(va); p_sk.append(kb); p_sv.append(vb); p_cv.append(conv_p)

        zs = rms_norm(xs, attn_norm_g[l]) @ w_in[l]
        qa2, ka2, va2, qb2, kb2, vb2, ga2, gb2 = split_projection(zs, pos_s)
        ka_all = jnp.concatenate([cache_diff_k[l].astype(ka2.dtype), ka2], axis=1)
        va_all = jnp.concatenate([cache_diff_v[l].astype(va2.dtype), va2], axis=1)
        kb_all = jnp.concatenate([cache_sb_k[l].astype(kb2.dtype), kb2], axis=1)
        vb_all = jnp.concatenate([cache_sb_v[l].astype(vb2.dtype), vb2], axis=1)
        oa2 = diff_attention(qa2, ka_all, va_all, pos_s, k_pos_s, lam)
        ob2 = stick_breaking(qb2, kb_all, vb_all, pos_s, k_pos_s)
        xs = xs + merge_branches(oa2, ob2, ga2, gb2, subln_g[l], lam_init,
                                 w_branch_a[l], w_branch_b[l], w_out[l])
        xs, conv_s = conv_ffn(xs, state_conv[l], ffn_norm_g[l], w_up[l], conv_w[l], conv_b[l], w_down[l])
        s_dk.append(ka2); s_dv.append(va2); s_sk.append(kb2); s_sv.append(vb2); s_cv.append(conv_s)

    y_prompt = rms_norm(xp, final_norm_g)
    y_sample = rms_norm(xs, final_norm_g)
    return (y_prompt, y_sample,
            jnp.stack(p_dk), jnp.stack(p_dv), jnp.stack(p_sk), jnp.stack(p_sv), jnp.stack(p_cv),
            jnp.stack(s_dk), jnp.stack(s_dv), jnp.stack(s_sk), jnp.stack(s_sv), jnp.stack(s_cv))
```

```python
import functools

import jax
import jax.numpy as jnp
from jax import lax
from jax.experimental import pallas as pl
from jax.experimental.pallas import tpu as pltpu

F32 = jnp.float32
BF16 = jnp.bfloat16

CHUNK = 64
ROPE_THETA = 10000.0
NORM_EPS = 1e-6
NEG_INF = -1e30
LAMBDA_INIT_LAYER0 = 0.8 - 0.6 * 1.0

LANES = 128
VMEM_LIMIT_BYTES = 56 * 1024 * 1024

PROJ_ROWS = 512
ATTN_ROWS = 256
FFN_ROWS = 512
FFN_COLS = 256
DECODE_KEYS = 256


def _rms(x, g):
    return x * lax.rsqrt(jnp.mean(x * x, axis=-1, keepdims=True) + NORM_EPS) * g


def _dot(a, b):
    return jnp.dot(a, b, preferred_element_type=F32)


def _dot_nt(a, b):
    return lax.dot_general(a, b, (((1,), (1,)), ((), ())), preferred_element_type=F32)


def _log2(n):
    assert n > 0 and n & (n - 1) == 0, n
    return n.bit_length() - 1


def _mod_pow2(x, n):
    _log2(n)
    return jnp.bitwise_and(x, n - 1)


def _div_pow2(x, n):
    return jnp.right_shift(x, _log2(n))


def _const_spec(shape):
    return pl.BlockSpec(shape, lambda *_: (0,) * len(shape), pipeline_mode=pl.Buffered(1))


def _params(n_axes):
    return pltpu.CompilerParams(dimension_semantics=("arbitrary",) * n_axes,
                                vmem_limit_bytes=VMEM_LIMIT_BYTES)


def _in_proj_body(x_ref, g_ref, w_ref, cos_ref, sin_ref,
                  kd_ref, vd_ref, ks_ref, vs_ref,
                  qd_bf, kd_bf, vd_bf, qs_bf, ks_bf, vs_bf, ga_ref, gb_ref,
                  *, width, d_model, head_dim, q_scale):
    xb = _rms(x_ref[...], g_ref[...]).astype(BF16)

    def proj(c0, n):
        return _dot(xb, w_ref[:, c0:c0 + n])

    cos = cos_ref[...]
    sin = sin_ref[...]
    lane = lax.broadcasted_iota(jnp.int32, cos.shape, 1)
    half = head_dim // 2
    first_half = _mod_pow2(lane, head_dim) < half

    def rope(z):
        rot = jnp.where(first_half, pltpu.roll(z, LANES - half, 1), pltpu.roll(z, half, 1))
        return z * cos + rot * sin

    zq = proj(0, width)
    zk = proj(width, width)
    for c in range(width // LANES):
        sl = slice(c * LANES, (c + 1) * LANES)
        qd_bf[:, sl] = (rope(zq[:, sl]) * q_scale).astype(BF16)
        rk = rope(zk[:, sl])
        kd_ref[:, sl] = rk
        kd_bf[:, sl] = rk.astype(BF16)
    zv = proj(2 * width, width)
    vd_ref[...] = zv
    vd_bf[...] = zv.astype(BF16)
    qs_bf[...] = (proj(3 * width, width) * q_scale).astype(BF16)
    zk = proj(4 * width, width)
    ks_ref[...] = zk
    ks_bf[...] = zk.astype(BF16)
    zv = proj(5 * width, width)
    vs_ref[...] = zv
    vs_bf[...] = zv.astype(BF16)
    for c in range(d_model // width):
        sl = slice(c * width, (c + 1) * width)
        ga_ref[:, sl] = jax.nn.sigmoid(proj(6 * width + c * width, width)).astype(BF16)
        gb_ref[:, sl] = jax.nn.sigmoid(proj(6 * width + d_model + c * width, width)).astype(BF16)


def _in_proj(x2d, g, w_bf, cos_t, sin_t, *, head_dim):
    n, d_model = x2d.shape
    n_in = w_bf.shape[1]
    width = d_model // 2
    tm = min(PROJ_ROWS, n)
    n_tab = cos_t.shape[0] // tm
    row = lambda i: (i, 0)
    tab = lambda i: (i % n_tab, 0)
    f32_out = jax.ShapeDtypeStruct((n, width), F32)
    bf_out = jax.ShapeDtypeStruct((n, width), BF16)
    gate_out = jax.ShapeDtypeStruct((n, d_model), BF16)
    body = functools.partial(_in_proj_body, width=width, d_model=d_model, head_dim=head_dim,
                             q_scale=head_dim ** -0.5)
    return pl.pallas_call(
        body,
        grid=(n // tm,),
        in_specs=[pl.BlockSpec((tm, d_model), row),
                  _const_spec((1, d_model)),
                  _const_spec((d_model, n_in)),
                  pl.BlockSpec((tm, LANES), tab),
                  pl.BlockSpec((tm, LANES), tab)],
        out_specs=[pl.BlockSpec((tm, width), row)] * 10 + [pl.BlockSpec((tm, d_model), row)] * 2,
        out_shape=[f32_out] * 4 + [bf_out] * 6 + [gate_out] * 2,
        compiler_params=_params(1),
        name="in_proj",
    )(x2d, g, w_bf, cos_t, sin_t)


def _rope_tables(pos, head_dim):
    half = head_dim // 2
    inv = ROPE_THETA ** (-jnp.arange(half, dtype=F32) * 2.0 / head_dim)
    ang = pos.astype(F32)[:, None] * inv[None, :]
    cos = jnp.cos(ang)
    sin = jnp.sin(ang)
    reps = LANES // head_dim
    return (jnp.tile(jnp.concatenate([cos, cos], -1), (1, reps)),
            jnp.tile(jnp.concatenate([-sin, sin], -1), (1, reps)))


def _stack_halves(q, half):
    lane = lax.broadcasted_iota(jnp.int32, q.shape, 1)
    zero = jnp.zeros_like(q)
    return jnp.concatenate([jnp.where(lane < half, q, zero), jnp.where(lane >= half, q, zero)], axis=0)


def _softmax_step(qs, k, v, m, l, acc, mask):
    s = _dot_nt(qs, k)
    if mask is not None:
        s = jnp.where(mask, s, NEG_INF)
    m_new = jnp.maximum(m, jnp.max(s, axis=1, keepdims=True))
    alpha = jnp.exp(m - m_new)
    p = jnp.exp(s - m_new)
    l_new = alpha * l + jnp.sum(p, axis=1, keepdims=True)
    acc_new = alpha * acc + _dot(p.astype(BF16), v)
    return m_new, l_new, acc_new


def _suffix_matrix(n):
    j = lax.broadcasted_iota(jnp.int32, (n, n), 0)
    s = lax.broadcasted_iota(jnp.int32, (n, n), 1)
    return jnp.where(j > s, 1.0, 0.0).astype(BF16)


def _stick_step(qs, k, v, r, acc, suffix, mask):
    z = _dot_nt(qs, k)
    sp = jnp.maximum(z, 0.0) + jnp.log1p(jnp.exp(-jnp.abs(z)))
    spm = sp if mask is None else jnp.where(mask, sp, 0.0)
    hi = spm.astype(BF16)
    lo = (spm - hi.astype(F32)).astype(BF16)
    later = _dot(hi, suffix) + _dot(lo, suffix)
    a = jnp.exp(z - sp - later - r)
    if mask is not None:
        a = jnp.where(mask, a, 0.0)
    acc_new = acc + _dot(a.astype(BF16), v)
    r_new = r + jnp.sum(spm, axis=1, keepdims=True)
    return r_new, acc_new


def _lambda_value(lq1, lk1, lq2, lk2):
    return (jnp.exp(jnp.sum(lq1[...] * lk1[...], axis=-1, keepdims=True))
            - jnp.exp(jnp.sum(lq2[...] * lk2[...], axis=-1, keepdims=True))
            + LAMBDA_INIT_LAYER0)


def _diff_finish(acc, l, lam, g, rows):
    o = acc / l
    o = o[:rows] - lam * o[rows:]
    return (_rms(o, g) * (1.0 - LAMBDA_INIT_LAYER0)).astype(BF16)


def _stick_finish(acc, rows, half):
    lane = lax.broadcasted_iota(jnp.int32, (rows, LANES), 1)
    return jnp.where(lane < half, acc[:rows], acc[rows:]).astype(BF16)


def _stacked_positions(rows, n_keys):
    t = _mod_pow2(lax.broadcasted_iota(jnp.int32, (2 * rows, n_keys), 0), rows)
    s = lax.broadcasted_iota(jnp.int32, (2 * rows, n_keys), 1)
    return t, s


def _diff_prompt_body(lq1, lk1, lq2, lk2, g_ref, q_ref, k_ref, v_ref, o_ref, acc_ref, *, rows, half):
    i = pl.program_id(2)
    qs = _stack_halves(q_ref[...], half)
    acc_ref[...] = jnp.zeros_like(acc_ref)
    m0 = jnp.full((2 * rows, 1), NEG_INF, F32)
    l0 = jnp.zeros((2 * rows, 1), F32)

    def block(j):
        off = pl.multiple_of(j * rows, rows)
        return k_ref[pl.ds(off, rows), :], v_ref[pl.ds(off, rows), :]

    def body(j, carry):
        m, l = carry
        k, v = block(j)
        m, l, acc = _softmax_step(qs, k, v, m, l, acc_ref[...], None)
        acc_ref[...] = acc
        return m, l

    m, l = lax.fori_loop(0, i, body, (m0, l0))
    t, s = _stacked_positions(rows, rows)
    k, v = block(i)
    m, l, acc = _softmax_step(qs, k, v, m, l, acc_ref[...],
                               _div_pow2(s, CHUNK) <= _div_pow2(t, CHUNK))
    o_ref[...] = _diff_finish(acc, l, _lambda_value(lq1, lk1, lq2, lk2), g_ref[...], rows)


def _stick_prompt_body(q_ref, k_ref, v_ref, o_ref, acc_ref, *, rows, half):
    i = pl.program_id(2)
    qs = _stack_halves(q_ref[...], half)
    suffix = _suffix_matrix(rows)

    def block(j):
        off = pl.multiple_of(j * rows, rows)
        return k_ref[pl.ds(off, rows), :], v_ref[pl.ds(off, rows), :]

    t, s = _stacked_positions(rows, rows)
    k, v = block(i)
    r, acc = _stick_step(qs, k, v, jnp.zeros((2 * rows, 1), F32), jnp.zeros((2 * rows, LANES), F32),
                         suffix, s < t)
    acc_ref[...] = acc

    def body(jj, r):
        k, v = block(i - 1 - jj)
        r, acc = _stick_step(qs, k, v, r, acc_ref[...], suffix, None)
        acc_ref[...] = acc
        return r

    lax.fori_loop(0, i, body, r)
    o_ref[...] = _stick_finish(acc_ref[...], rows, half)


def _prompt_attention(body, extra_inputs, extra_specs, q, k, v, *, batch, seq):
    n, width = q.shape
    rows = ATTN_ROWS
    nq = seq // rows
    groups = width // LANES
    q_spec = pl.BlockSpec((rows, LANES), lambda b, g, i: (b * nq + i, g))
    kv_spec = pl.BlockSpec((seq, LANES), lambda b, g, i: (b, g))
    return pl.pallas_call(
        body,
        grid=(batch, groups, nq),
        in_specs=extra_specs + [q_spec, kv_spec, kv_spec],
        out_specs=q_spec,
        out_shape=jax.ShapeDtypeStruct((n, width), BF16),
        scratch_shapes=[pltpu.VMEM((2 * rows, LANES), F32)],
        compiler_params=_params(3),
        name=body.func.__name__.strip("_"),
    )(*extra_inputs, q, k, v)


def _pad_rows(x, n):
    return jnp.concatenate([x, jnp.zeros((n - x.shape[0], x.shape[1]), x.dtype)], axis=0)


def _diff_decode_body(lq1, lk1, lq2, lk2, g_ref, q_ref, kn_ref, vn_ref, ck_ref, cv_ref, o_ref,
                      *, rows, half, past):
    lam = _lambda_value(lq1, lk1, lq2, lk2)
    n_blocks = past // DECODE_KEYS
    t, s = _stacked_positions(rows, LANES)
    new_mask = (s < rows) & (_div_pow2(past + s, CHUNK) <= _div_pow2(past + t, CHUNK))
    for g in range(q_ref.shape[1] // LANES):
        sl = slice(g * LANES, (g + 1) * LANES)
        qs = _stack_halves(q_ref[:, sl], half)

        def body(j, carry, sl=sl, qs=qs):
            off = pl.multiple_of(j * DECODE_KEYS, DECODE_KEYS)
            k = ck_ref[pl.ds(off, DECODE_KEYS), sl].astype(BF16)
            v = cv_ref[pl.ds(off, DECODE_KEYS), sl].astype(BF16)
            return _softmax_step(qs, k, v, *carry, None)

        carry = (jnp.full((2 * rows, 1), NEG_INF, F32), jnp.zeros((2 * rows, 1), F32),
                 jnp.zeros((2 * rows, LANES), F32))
        carry = lax.fori_loop(0, n_blocks, body, carry)
        m, l, acc = _softmax_step(qs, _pad_rows(kn_ref[:, sl], LANES), _pad_rows(vn_ref[:, sl], LANES),
                                  *carry, new_mask)
        o_ref[:, sl] = _diff_finish(acc, l, lam, g_ref[...], rows)


def _stick_decode_body(q_ref, kn_ref, vn_ref, ck_ref, cv_ref, o_ref, *, rows, half, past):
    n_blocks = past // DECODE_KEYS
    suffix = _suffix_matrix(DECODE_KEYS)
    suffix_new = _suffix_matrix(LANES)
    t, s = _stacked_positions(rows, LANES)
    new_mask = (s < rows) & (s < t)
    for g in range(q_ref.shape[1] // LANES):
        sl = slice(g * LANES, (g + 1) * LANES)
        qs = _stack_halves(q_ref[:, sl], half)
        carry = _stick_step(qs, _pad_rows(kn_ref[:, sl], LANES), _pad_rows(vn_ref[:, sl], LANES),
                            jnp.zeros((2 * rows, 1), F32), jnp.zeros((2 * rows, LANES), F32),
                            suffix_new, new_mask)

        def body(jj, carry, sl=sl, qs=qs):
            off = pl.multiple_of((n_blocks - 1 - jj) * DECODE_KEYS, DECODE_KEYS)
            k = ck_ref[pl.ds(off, DECODE_KEYS), sl].astype(BF16)
            v = cv_ref[pl.ds(off, DECODE_KEYS), sl].astype(BF16)
            return _stick_step(qs, k, v, *carry, suffix, None)

        _, acc = lax.fori_loop(0, n_blocks, body, carry)
        o_ref[:, sl] = _stick_finish(acc, rows, half)


def _decode_attention(body, extra_inputs, extra_specs, q, k_new, v_new, cache_k, cache_v, *, batch, rows):
    n, width = q.shape
    past = cache_k.shape[0] // batch
    new_spec = pl.BlockSpec((rows, width), lambda b: (b, 0))
    cache_spec = pl.BlockSpec((past, width), lambda b: (b, 0))
    return pl.pallas_call(
        functools.partial(body, past=past),
        grid=(batch,),
        in_specs=extra_specs + [new_spec, new_spec, new_spec, cache_spec, cache_spec],
        out_specs=new_spec,
        out_shape=jax.ShapeDtypeStruct((n, width), BF16),
        compiler_params=_params(1),
        name=body.func.__name__.strip("_"),
    )(*extra_inputs, q, k_new, v_new, cache_k, cache_v)


def _merge_ffn_body(*refs, d_ff, seq_rows):
    (x_ref, oa_ref, ob_ref, ga_ref, gb_ref, wa_ref, wb_ref, wo_ref, fg_ref, wup_ref, cw_ref, cb_ref,
     wdn_ref, ng_ref) = refs[:14]
    if seq_rows is None:
        y_ref, conv_ref, acc_ref, carry_ref = refs[14:]
    else:
        h0_ref, h1_ref, y_ref, last2_ref, last1_ref, acc_ref, u_ref = refs[14:]
    tm = x_ref.shape[0]
    assert cw_ref.shape[0] == 3

    mix = (ga_ref[...].astype(F32) * _dot(oa_ref[...], wa_ref[...])
           + gb_ref[...].astype(F32) * _dot(ob_ref[...], wb_ref[...]))
    x1 = x_ref[...] + _dot(mix.astype(BF16), wo_ref[...])
    xn = _rms(x1, fg_ref[...]).astype(BF16)

    if seq_rows is None:
        @pl.when(pl.program_id(1) == 0)
        def _():
            carry_ref[...] = jnp.zeros_like(carry_ref)
        pos = lax.broadcasted_iota(jnp.int32, (tm, FFN_COLS), 0)
    else:
        pos = _mod_pow2(lax.broadcasted_iota(jnp.int32, (tm, FFN_COLS), 0), seq_rows)

    def conv(cols):
        u = _dot(xn, wup_ref[:, cols])
        if seq_rows is None:
            prev2 = carry_ref[6:7, cols]
            prev1 = carry_ref[7:8, cols]
            carry_ref[:, cols] = u[tm - 8:, :]
            conv_ref[:, cols] = u[tm - 8:, :]
        else:
            n_seq = tm // seq_rows
            expand = lambda h: jnp.broadcast_to(h[:, None, :], (n_seq, seq_rows, FFN_COLS)).reshape(tm, FFN_COLS)
            prev2 = expand(h0_ref[:, cols])
            prev1 = expand(h1_ref[:, cols])
            for part in range(FFN_COLS // LANES):
                lanes = slice(part * LANES, (part + 1) * LANES)
                out = slice(cols.start + part * LANES, cols.start + (part + 1) * LANES)
                u_ref[...] = u[:, lanes]
                last2_ref[:, out] = u_ref[pl.ds(seq_rows - 2, n_seq, stride=seq_rows), :]
                last1_ref[:, out] = u_ref[pl.ds(seq_rows - 1, n_seq, stride=seq_rows), :]
        u1 = jnp.where(pos == 0, prev1, pltpu.roll(u, 1, 0))
        u2 = jnp.where(pos == 0, prev2, jnp.where(pos == 1, prev1, pltpu.roll(u, 2, 0)))
        return cb_ref[:, cols] + cw_ref[0:1, cols] * u2 + cw_ref[1:2, cols] * u1 + cw_ref[2:3, cols] * u

    for c in range(d_ff // FFN_COLS):
        gate = conv(slice(c * FFN_COLS, (c + 1) * FFN_COLS))
        val = conv(slice(d_ff + c * FFN_COLS, d_ff + (c + 1) * FFN_COLS))
        h = (gate * jax.nn.sigmoid(gate) * val).astype(BF16)
        part = _dot(h, wdn_ref[c * FFN_COLS:(c + 1) * FFN_COLS, :])
        if c == 0:
            acc_ref[...] = part
        else:
            acc_ref[...] += part
    y_ref[...] = _rms(x1 + acc_ref[...], ng_ref[...])


def _merge_ffn(x2d, oa, ob, ga, gb, wa, wb, wo, fg, wup, cw, cb, wdn, ng, *, batch, history=None):
    n, d_model = x2d.shape
    width = oa.shape[1]
    d_ff = wdn.shape[0]
    two_ff = wup.shape[1]
    seq = n // batch
    consts = [wa, wb, wo, fg, wup, cw, cb, wdn, ng]
    const_specs = [_const_spec(a.shape) for a in consts]
    if history is None:
        tm = min(FFN_ROWS, seq)
        nt = seq // tm
        grid = (batch, nt)
        row = lambda b, j: (b * nt + j, 0)
        extra, extra_specs = [], []
        conv_shape = [jax.ShapeDtypeStruct((batch, 8, two_ff), F32)]
        conv_spec = [pl.BlockSpec((None, 8, two_ff), lambda b, j: (b, 0, 0))]
        scratch = [pltpu.VMEM((tm, d_model), F32), pltpu.VMEM((8, two_ff), F32)]
        seq_rows = None
    else:
        tm = n
        grid = (1,)
        row = lambda i: (0, 0)
        extra = list(history)
        extra_specs = [_const_spec(h.shape) for h in history]
        conv_shape = [jax.ShapeDtypeStruct((batch, two_ff), F32)] * 2
        conv_spec = [pl.BlockSpec((batch, two_ff), row)] * 2
        scratch = [pltpu.VMEM((tm, d_model), F32), pltpu.VMEM((tm, LANES), F32)]
        seq_rows = seq
    tile = lambda cols: pl.BlockSpec((tm, cols), row)
    return pl.pallas_call(
        functools.partial(_merge_ffn_body, d_ff=d_ff, seq_rows=seq_rows),
        grid=grid,
        in_specs=[tile(d_model), tile(width), tile(width), tile(d_model), tile(d_model)]
                 + const_specs + extra_specs,
        out_specs=[tile(d_model)] + conv_spec,
        out_shape=[jax.ShapeDtypeStruct((n, d_model), F32)] + conv_shape,
        scratch_shapes=scratch,
        compiler_params=_params(len(grid)),
        name="merge_ffn" if history is None else "merge_ffn_streams",
    )(x2d, oa, ob, ga, gb, *consts, *extra)


def kernel(x_prompt, x_sample, cache_diff_k, cache_diff_v, cache_sb_k, cache_sb_v, state_conv,
           attn_norm_g, w_in, lambda_q1, lambda_k1, lambda_q2, lambda_k2, subln_g, w_branch_a,
           w_branch_b, w_out, ffn_norm_g, w_up, conv_w, conv_b, w_down, final_norm_g):
    depth = w_in.shape[0]
    assert depth == 1, "single-layer trunk only"
    batch, seq, d_model = x_prompt.shape
    dec_batch, dec_seq, _ = x_sample.shape
    past = cache_diff_k.shape[2]
    width = d_model // 2
    da_heads, _, da_dim = cache_diff_k.shape[3:]
    da_v = cache_diff_v.shape[-1]
    sb_heads, sb_dim = cache_sb_k.shape[3:]
    two_ff = w_up.shape[-1]
    assert da_dim == sb_dim and 2 * da_dim == LANES and da_v == LANES
    half = da_dim

    bf = lambda w: w[0].astype(BF16)
    row1 = lambda v: v.reshape(1, -1)
    w_in_bf = bf(w_in)
    consts = (bf(w_branch_a), bf(w_branch_b), bf(w_out), ffn_norm_g, bf(w_up), conv_w[0], conv_b,
              bf(w_down), row1(final_norm_g))
    lams = [row1(v[0]) for v in (lambda_q1, lambda_k1, lambda_q2, lambda_k2)] + [row1(subln_g[0])]
    lam_specs = [_const_spec(v.shape) for v in lams]

    diff_prompt = functools.partial(_diff_prompt_body, rows=ATTN_ROWS, half=half)
    stick_prompt = functools.partial(_stick_prompt_body, rows=ATTN_ROWS, half=half)
    diff_decode = functools.partial(_diff_decode_body, rows=dec_seq, half=half)
    stick_decode = functools.partial(_stick_decode_body, rows=dec_seq, half=half)

    xp = x_prompt.reshape(batch * seq, d_model)
    cos_p, sin_p = _rope_tables(jnp.arange(seq), da_dim)
    (p_dk, p_dv, p_sk, p_sv, qd, kd, vd, qs, ks, vs, ga, gb) = _in_proj(
        xp, attn_norm_g, w_in_bf, cos_p, sin_p, head_dim=da_dim)
    oa = _prompt_attention(diff_prompt, lams, lam_specs, qd, kd, vd, batch=batch, seq=seq)
    ob = _prompt_attention(stick_prompt, [], [], qs, ks, vs, batch=batch, seq=seq)
    y_p, conv_p = _merge_ffn(xp, oa, ob, ga, gb, *consts, batch=batch)

    xs = x_sample.reshape(dec_batch * dec_seq, d_model)
    cos_s, sin_s = _rope_tables(past + jnp.arange(dec_seq), da_dim)
    cos_s, sin_s = jnp.tile(cos_s, (dec_batch, 1)), jnp.tile(sin_s, (dec_batch, 1))
    (s_dk, s_dv, s_sk, s_sv, qd, kd, vd, qs, ks, vs, ga, gb) = _in_proj(
        xs, attn_norm_g, w_in_bf, cos_s, sin_s, head_dim=da_dim)
    flat = lambda c: c.reshape(dec_batch * past, width)
    oa = _decode_attention(diff_decode, lams, lam_specs, qd, kd, vd, flat(cache_diff_k), flat(cache_diff_v),
                           batch=dec_batch, rows=dec_seq)
    ob = _decode_attention(stick_decode, [], [], qs, ks, vs, flat(cache_sb_k), flat(cache_sb_v),
                           batch=dec_batch, rows=dec_seq)
    y_s, last2, last1 = _merge_ffn(xs, oa, ob, ga, gb, *consts, batch=dec_batch,
                                   history=(state_conv[0, :, 0], state_conv[0, :, 1]))

    conv_p = conv_p[:, 6:]
    conv_s = jnp.stack([last2, last1], axis=1)
    return (y_p.reshape(batch, seq, d_model), y_s.reshape(dec_batch, dec_seq, d_model),
            p_dk.reshape(depth, batch, seq, da_heads, 2, da_dim),
            p_dv.reshape(depth, batch, seq, da_heads, da_v),
            p_sk.reshape(depth, batch, seq, sb_heads, sb_dim),
            p_sv.reshape(depth, batch, seq, sb_heads, sb_dim),
            conv_p[None],
            s_dk.reshape(depth, dec_batch, dec_seq, da_heads, 2, da_dim),
            s_dv.reshape(depth, dec_batch, dec_seq, da_heads, da_v),
            s_sk.reshape(depth, dec_batch, dec_seq, sb_heads, sb_dim),
            s_sv.reshape(depth, dec_batch, dec_seq, sb_heads, sb_dim),
            conv_s[None])
```

```python
import functools

import jax
import jax.numpy as jnp
from jax import lax
from jax.experimental import pallas as pl
from jax.experimental.pallas import tpu as pltpu

F32 = jnp.float32
BF16 = jnp.bfloat16

CHUNK = 64
ROPE_THETA = 10000.0
NORM_EPS = 1e-6
NEG_INF = -1e30
LAMBDA_INIT_LAYER0 = 0.8 - 0.6 * 1.0

LANES = 128
VMEM_LIMIT_BYTES = 56 * 1024 * 1024

PROJ_ROWS = 512
ATTN_ROWS = 256
FFN_ROWS = 512
FFN_COLS = 256
DECODE_KEYS = 256


def _rms(x, g):
    return x * lax.rsqrt(jnp.mean(x * x, axis=-1, keepdims=True) + NORM_EPS) * g


def _dot(a, b):
    return jnp.dot(a, b, preferred_element_type=F32)


def _dot_nt(a, b):
    return lax.dot_general(a, b, (((1,), (1,)), ((), ())), preferred_element_type=F32)


def _log2(n):
    assert n > 0 and n & (n - 1) == 0, n
    return n.bit_length() - 1


def _mod_pow2(x, n):
    _log2(n)
    return jnp.bitwise_and(x, n - 1)


def _div_pow2(x, n):
    return jnp.right_shift(x, _log2(n))


def _lanes(g):
    return slice(g * LANES, (g + 1) * LANES)


def _const_spec(shape):
    return pl.BlockSpec(shape, lambda *_: (0,) * len(shape), pipeline_mode=pl.Buffered(1))


def _params(n_axes):
    return pltpu.CompilerParams(dimension_semantics=("arbitrary",) * n_axes,
                                vmem_limit_bytes=VMEM_LIMIT_BYTES)


def _in_proj_body(x_ref, g_ref, w_ref, cos_ref, sin_ref,
                  kd_ref, vd_ref, ks_ref, vs_ref,
                  qd_bf, kd_bf, vd_bf, qs_bf, ks_bf, vs_bf, ga_ref, gb_ref,
                  *, width, d_model, head_dim, q_scale):
    xb = _rms(x_ref[...], g_ref[...]).astype(BF16)

    def proj(c0, n):
        return _dot(xb, w_ref[:, c0:c0 + n])

    cos = cos_ref[...]
    sin = sin_ref[...]
    lane = lax.broadcasted_iota(jnp.int32, cos.shape, 1)
    half = head_dim // 2
    first_half = _mod_pow2(lane, head_dim) < half

    def rope(z):
        rot = jnp.where(first_half, pltpu.roll(z, LANES - half, 1), pltpu.roll(z, half, 1))
        return z * cos + rot * sin

    zq = proj(0, width)
    zk = proj(width, width)
    for c in range(width // LANES):
        sl = _lanes(c)
        qd_bf[:, sl] = (rope(zq[:, sl]) * q_scale).astype(BF16)
        rk = rope(zk[:, sl])
        kd_ref[:, sl] = rk
        kd_bf[:, sl] = rk.astype(BF16)
    zv = proj(2 * width, width)
    vd_ref[...] = zv
    vd_bf[...] = zv.astype(BF16)
    qs_bf[...] = (proj(3 * width, width) * q_scale).astype(BF16)
    zk = proj(4 * width, width)
    ks_ref[...] = zk
    ks_bf[...] = zk.astype(BF16)
    zv = proj(5 * width, width)
    vs_ref[...] = zv
    vs_bf[...] = zv.astype(BF16)
    for c in range(d_model // width):
        sl = slice(c * width, (c + 1) * width)
        ga_ref[:, sl] = jax.nn.sigmoid(proj(6 * width + c * width, width)).astype(BF16)
        gb_ref[:, sl] = jax.nn.sigmoid(proj(6 * width + d_model + c * width, width)).astype(BF16)


def _in_proj(x2d, g, w_bf, cos_t, sin_t, *, head_dim):
    n, d_model = x2d.shape
    n_in = w_bf.shape[1]
    width = d_model // 2
    tm = min(PROJ_ROWS, n)
    n_tab = cos_t.shape[0] // tm
    row = lambda i: (i, 0)
    tab = lambda i: (i % n_tab, 0)
    f32_out = jax.ShapeDtypeStruct((n, width), F32)
    bf_out = jax.ShapeDtypeStruct((n, width), BF16)
    gate_out = jax.ShapeDtypeStruct((n, d_model), BF16)
    body = functools.partial(_in_proj_body, width=width, d_model=d_model, head_dim=head_dim,
                             q_scale=head_dim ** -0.5)
    return pl.pallas_call(
        body,
        grid=(n // tm,),
        in_specs=[pl.BlockSpec((tm, d_model), row),
                  _const_spec((1, d_model)),
                  _const_spec((d_model, n_in)),
                  pl.BlockSpec((tm, LANES), tab),
                  pl.BlockSpec((tm, LANES), tab)],
        out_specs=[pl.BlockSpec((tm, width), row)] * 10 + [pl.BlockSpec((tm, d_model), row)] * 2,
        out_shape=[f32_out] * 4 + [bf_out] * 6 + [gate_out] * 2,
        compiler_params=_params(1),
        name="in_proj",
    )(x2d, g, w_bf, cos_t, sin_t)


def _rope_tables(pos, head_dim):
    half = head_dim // 2
    inv = ROPE_THETA ** (-jnp.arange(half, dtype=F32) * 2.0 / head_dim)
    ang = pos.astype(F32)[:, None] * inv[None, :]
    cos = jnp.cos(ang)
    sin = jnp.sin(ang)
    reps = LANES // head_dim
    return (jnp.tile(jnp.concatenate([cos, cos], -1), (1, reps)),
            jnp.tile(jnp.concatenate([-sin, sin], -1), (1, reps)))


def _stack_halves(q, half):
    lane = lax.broadcasted_iota(jnp.int32, q.shape, 1)
    zero = jnp.zeros_like(q)
    return jnp.concatenate([jnp.where(lane < half, q, zero), jnp.where(lane >= half, q, zero)], axis=0)


def _softplus(z):
    neg_abs = lax.bitcast_convert_type(
        lax.bitcast_convert_type(z, jnp.uint32) | jnp.uint32(0x80000000), F32)
    return jnp.maximum(z, 0.0) + jnp.log(1.0 + jnp.exp(neg_abs))


def _split_bf16(x):
    hi = lax.bitcast_convert_type(
        lax.bitcast_convert_type(x, jnp.uint32) & jnp.uint32(0xFFFF0000), F32)
    return hi.astype(BF16), (x - hi).astype(BF16)


def _suffix_matrix(n, keys_on_rows):
    a = lax.broadcasted_iota(jnp.int32, (n, n), 0)
    b = lax.broadcasted_iota(jnp.int32, (n, n), 1)
    after = (b > a) if keys_on_rows else (a > b)
    return jnp.where(after, 1.0, 0.0).astype(BF16)


def _lambda_value(lq1, lk1, lq2, lk2):
    return (jnp.exp(jnp.sum(lq1[...] * lk1[...], axis=-1, keepdims=True))
            - jnp.exp(jnp.sum(lq2[...] * lk2[...], axis=-1, keepdims=True))
            + LAMBDA_INIT_LAYER0)


def _softmax_step_t(k, qs, vt, m, l, acc, mask):
    s = _dot_nt(k, qs)
    if mask is not None:
        s = jnp.where(mask, s, NEG_INF)
    m_new = jnp.maximum(m, jnp.max(s, axis=0, keepdims=True))
    alpha = jnp.exp(m - m_new)
    p = jnp.exp(s - m_new)
    l_new = alpha * l + jnp.sum(p, axis=0, keepdims=True)
    acc_new = alpha * acc + _dot(vt, p.astype(BF16))
    return m_new, l_new, acc_new


def _stick_step_t(k, qs, vt, r, acc, suffix, mask):
    z = _dot_nt(k, qs)
    sp = _softplus(z)
    spm = sp if mask is None else jnp.where(mask, sp, 0.0)
    hi, lo = _split_bf16(spm)
    later = _dot(suffix, hi) + _dot(suffix, lo)
    a = jnp.exp(z - sp - later - r)
    if mask is not None:
        a = jnp.where(mask, a, 0.0)
    acc_new = acc + _dot(vt, a.astype(BF16))
    r_new = r + jnp.sum(spm, axis=0, keepdims=True)
    return r_new, acc_new


def _key_query_positions(n_keys, rows):
    s = lax.broadcasted_iota(jnp.int32, (n_keys, 2 * rows), 0)
    t = _mod_pow2(lax.broadcasted_iota(jnp.int32, (n_keys, 2 * rows), 1), rows)
    return s, t


def _diff_finish_t(acc, l, lam, g_col, rows):
    o = acc / l
    o = o[:, :rows] - lam * o[:, rows:]
    y = o * lax.rsqrt(jnp.mean(o * o, axis=0, keepdims=True) + NORM_EPS) * g_col
    return (y * (1.0 - LAMBDA_INIT_LAYER0)).T.astype(BF16)


def _stick_finish_t(acc, rows, half):
    row = lax.broadcasted_iota(jnp.int32, (LANES, rows), 0)
    return jnp.where(row < half, acc[:, :rows], acc[:, rows:]).T.astype(BF16)


def _transpose_values(v_ref, vt_ref):
    nblk, _, n = vt_ref.shape
    for c in range(nblk):
        vt_ref[c] = v_ref[c * n:(c + 1) * n, :].astype(F32).T.astype(BF16)


def _diff_prompt_body(lq1, lk1, lq2, lk2, g_ref, q_ref, k_ref, v_ref, o_ref, vt_ref, acc_ref, *, rows, half):
    i = pl.program_id(1)
    groups = q_ref.shape[1] // LANES

    @pl.when(i == 0)
    def _():
        _transpose_values(v_ref, vt_ref)

    qs = [_stack_halves(q_ref[:, _lanes(g)], half) for g in range(groups)]
    acc_ref[...] = jnp.zeros_like(acc_ref)

    def step(j, stats, mask):
        off = pl.multiple_of(j * rows, rows)
        out = []
        for g in range(groups):
            m, l, acc = _softmax_step_t(k_ref[pl.ds(off, rows), _lanes(g)], qs[g], vt_ref[j, _lanes(g), :],
                                        *stats[g], acc_ref[g], mask)
            acc_ref[g] = acc
            out.append((m, l))
        return tuple(out)

    init = tuple((jnp.full((1, 2 * rows), NEG_INF, F32), jnp.zeros((1, 2 * rows), F32))
                 for _ in range(groups))
    stats = lax.fori_loop(0, i, lambda j, st: step(j, st, None), init)
    s, t = _key_query_positions(rows, rows)
    stats = step(i, stats, _div_pow2(s, CHUNK) <= _div_pow2(t, CHUNK))
    lam = _lambda_value(lq1, lk1, lq2, lk2)
    for g in range(groups):
        o_ref[:, _lanes(g)] = _diff_finish_t(acc_ref[g], stats[g][1], lam, g_ref[...], rows)


def _stick_prompt_body(q_ref, k_ref, v_ref, o_ref, vt_ref, acc_ref, *, rows, half):
    i = pl.program_id(1)
    groups = q_ref.shape[1] // LANES

    @pl.when(i == 0)
    def _():
        _transpose_values(v_ref, vt_ref)

    qs = [_stack_halves(q_ref[:, _lanes(g)], half) for g in range(groups)]
    suffix = _suffix_matrix(rows, keys_on_rows=True)
    acc_ref[...] = jnp.zeros_like(acc_ref)

    def step(j, carry, mask):
        off = pl.multiple_of(j * rows, rows)
        out = []
        for g in range(groups):
            r, acc = _stick_step_t(k_ref[pl.ds(off, rows), _lanes(g)], qs[g], vt_ref[j, _lanes(g), :],
                                   carry[g], acc_ref[g], suffix, mask)
            acc_ref[g] = acc
            out.append(r)
        return tuple(out)

    s, t = _key_query_positions(rows, rows)
    carry = step(i, tuple(jnp.zeros((1, 2 * rows), F32) for _ in range(groups)), s < t)
    lax.fori_loop(0, i, lambda jj, c: step(i - 1 - jj, c, None), carry)
    for g in range(groups):
        o_ref[:, _lanes(g)] = _stick_finish_t(acc_ref[g], rows, half)


def _prompt_attention(body, name, extra_inputs, extra_specs, q, k, v, *, batch, seq):
    n, width = q.shape
    rows = ATTN_ROWS
    nq = seq // rows
    q_spec = pl.BlockSpec((rows, width), lambda b, i: (b * nq + i, 0))
    kv_spec = pl.BlockSpec((seq, width), lambda b, i: (b, 0))
    return pl.pallas_call(
        body,
        grid=(batch, nq),
        in_specs=extra_specs + [q_spec, kv_spec, kv_spec],
        out_specs=q_spec,
        out_shape=jax.ShapeDtypeStruct((n, width), BF16),
        scratch_shapes=[pltpu.VMEM((nq, width, rows), BF16),
                        pltpu.VMEM((width // LANES, LANES, 2 * rows), F32)],
        compiler_params=_params(2),
        name=name,
    )(*extra_inputs, q, k, v)


def _pad_rows(x, n):
    return jnp.concatenate([x, jnp.zeros((n - x.shape[0], x.shape[1]), x.dtype)], axis=0)


def _stacked_positions(rows, n_keys):
    t = _mod_pow2(lax.broadcasted_iota(jnp.int32, (2 * rows, n_keys), 0), rows)
    s = lax.broadcasted_iota(jnp.int32, (2 * rows, n_keys), 1)
    return t, s


def _diff_decode_body(lq1, lk1, lq2, lk2, g_ref, q_ref, kn_ref, vn_ref, ckt_ref, cv_ref, o_ref,
                      *, rows, half):
    past = ckt_ref.shape[1]
    lam = _lambda_value(lq1, lk1, lq2, lk2)
    t, s = _stacked_positions(rows, LANES)
    new_mask = (s < rows) & (_div_pow2(past + s, CHUNK) <= _div_pow2(past + t, CHUNK))
    for g in range(q_ref.shape[1] // LANES):
        sl = _lanes(g)
        qs = _stack_halves(q_ref[:, sl], half)
        s_c = _dot(qs, ckt_ref[sl, :].astype(BF16))
        s_n = jnp.where(new_mask, _dot_nt(qs, _pad_rows(kn_ref[:, sl], LANES)), NEG_INF)
        m = jnp.maximum(jnp.max(s_c, axis=1, keepdims=True), jnp.max(s_n, axis=1, keepdims=True))
        p_c = jnp.exp(s_c - m)
        p_n = jnp.exp(s_n - m)
        l = jnp.sum(p_c, axis=1, keepdims=True) + jnp.sum(p_n, axis=1, keepdims=True)
        acc = (_dot(p_c.astype(BF16), cv_ref[:, g, :].astype(BF16))
               + _dot(p_n.astype(BF16), _pad_rows(vn_ref[:, sl], LANES)))
        o = acc / l
        o = o[:rows] - lam * o[rows:]
        o_ref[:, sl] = (_rms(o, g_ref[...]) * (1.0 - LAMBDA_INIT_LAYER0)).astype(BF16)


def _stick_decode_body(q_ref, kn_ref, vn_ref, ckt_ref, cvt_ref, o_ref, *, rows, half):
    past = ckt_ref.shape[1]
    suffix = _suffix_matrix(DECODE_KEYS, keys_on_rows=False)
    suffix_new = _suffix_matrix(LANES, keys_on_rows=False)
    t, s = _stacked_positions(rows, LANES)
    new_mask = (s < rows) & (s < t)
    lane = lax.broadcasted_iota(jnp.int32, (rows, LANES), 1)
    for g in range(q_ref.shape[1] // LANES):
        sl = _lanes(g)
        qs = _stack_halves(q_ref[:, sl], half)
        z = _dot_nt(qs, _pad_rows(kn_ref[:, sl], LANES))
        sp = _softplus(z)
        spm = jnp.where(new_mask, sp, 0.0)
        hi, lo = _split_bf16(spm)
        a = jnp.where(new_mask, jnp.exp(z - sp - _dot(hi, suffix_new) - _dot(lo, suffix_new)), 0.0)
        acc = _dot(a.astype(BF16), _pad_rows(vn_ref[:, sl], LANES))
        carry = jnp.sum(spm, axis=1, keepdims=True)
        z = _dot(qs, ckt_ref[sl, :].astype(BF16))
        sp = _softplus(z)
        hi, lo = _split_bf16(sp)
        vt = cvt_ref[sl, :].astype(BF16)
        for blk in reversed(range(past // DECODE_KEYS)):
            cols = slice(blk * DECODE_KEYS, (blk + 1) * DECODE_KEYS)
            later = _dot(hi[:, cols], suffix) + _dot(lo[:, cols], suffix)
            a = jnp.exp(z[:, cols] - sp[:, cols] - later - carry)
            acc = acc + _dot_nt(a.astype(BF16), vt[:, cols])
            carry = carry + jnp.sum(sp[:, cols], axis=1, keepdims=True)
        o_ref[:, sl] = jnp.where(lane < half, acc[:rows], acc[rows:]).astype(BF16)


def _decode_attention(body, name, extra_inputs, extra_specs, q, k_new, v_new, cache_kt, cache_v, *, rows):
    n, width = q.shape
    new_spec = pl.BlockSpec((rows, width), lambda b: (b, 0))
    cache_spec = lambda c: pl.BlockSpec((None,) + c.shape[1:], lambda b: (b,) + (0,) * (c.ndim - 1))
    return pl.pallas_call(
        body,
        grid=(n // rows,),
        in_specs=extra_specs + [new_spec, new_spec, new_spec, cache_spec(cache_kt), cache_spec(cache_v)],
        out_specs=new_spec,
        out_shape=jax.ShapeDtypeStruct((n, width), BF16),
        compiler_params=_params(1),
        name=name,
    )(*extra_inputs, q, k_new, v_new, cache_kt, cache_v)


def _merge_ffn_body(*refs, d_ff, seq_rows):
    (x_ref, oa_ref, ob_ref, ga_ref, gb_ref, wa_ref, wb_ref, wo_ref, fg_ref, wup_ref, cw_ref, cb_ref,
     wdn_ref, ng_ref) = refs[:14]
    if seq_rows is None:
        y_ref, conv_ref, acc_ref, carry_ref = refs[14:]
    else:
        h0_ref, h1_ref, y_ref, last2_ref, last1_ref, acc_ref, u_ref = refs[14:]
    tm = x_ref.shape[0]
    assert cw_ref.shape[0] == 3

    mix = (ga_ref[...].astype(F32) * _dot(oa_ref[...], wa_ref[...])
           + gb_ref[...].astype(F32) * _dot(ob_ref[...], wb_ref[...]))
    x1 = x_ref[...] + _dot(mix.astype(BF16), wo_ref[...])
    xn = _rms(x1, fg_ref[...]).astype(BF16)

    if seq_rows is None:
        @pl.when(pl.program_id(1) == 0)
        def _():
            carry_ref[...] = jnp.zeros_like(carry_ref)
        pos = lax.broadcasted_iota(jnp.int32, (tm, FFN_COLS), 0)
    else:
        pos = _mod_pow2(lax.broadcasted_iota(jnp.int32, (tm, FFN_COLS), 0), seq_rows)

    def conv(cols):
        u = _dot(xn, wup_ref[:, cols])
        if seq_rows is None:
            prev2 = carry_ref[6:7, cols]
            prev1 = carry_ref[7:8, cols]
            carry_ref[:, cols] = u[tm - 8:, :]
            conv_ref[:, cols] = u[tm - 8:, :]
        else:
            n_seq = tm // seq_rows
            expand = lambda h: jnp.broadcast_to(h[:, None, :], (n_seq, seq_rows, FFN_COLS)).reshape(tm, FFN_COLS)
            prev2 = expand(h0_ref[:, cols])
            prev1 = expand(h1_ref[:, cols])
            for part in range(FFN_COLS // LANES):
                lanes = _lanes(part)
                out = slice(cols.start + part * LANES, cols.start + (part + 1) * LANES)
                u_ref[...] = u[:, lanes]
                last2_ref[:, out] = u_ref[pl.ds(seq_rows - 2, n_seq, stride=seq_rows), :]
                last1_ref[:, out] = u_ref[pl.ds(seq_rows - 1, n_seq, stride=seq_rows), :]
        u1 = jnp.where(pos == 0, prev1, pltpu.roll(u, 1, 0))
        u2 = jnp.where(pos == 0, prev2, jnp.where(pos == 1, prev1, pltpu.roll(u, 2, 0)))
        return cb_ref[:, cols] + cw_ref[0:1, cols] * u2 + cw_ref[1:2, cols] * u1 + cw_ref[2:3, cols] * u

    for c in range(d_ff // FFN_COLS):
        gate = conv(slice(c * FFN_COLS, (c + 1) * FFN_COLS))
        val = conv(slice(d_ff + c * FFN_COLS, d_ff + (c + 1) * FFN_COLS))
        h = (gate * jax.nn.sigmoid(gate) * val).astype(BF16)
        part = _dot(h, wdn_ref[c * FFN_COLS:(c + 1) * FFN_COLS, :])
        if c == 0:
            acc_ref[...] = part
        else:
            acc_ref[...] += part
    y_ref[...] = _rms(x1 + acc_ref[...], ng_ref[...])


def _merge_ffn(x2d, oa, ob, ga, gb, wa, wb, wo, fg, wup, cw, cb, wdn, ng, *, batch, history=None):
    n, d_model = x2d.shape
    width = oa.shape[1]
    d_ff = wdn.shape[0]
    two_ff = wup.shape[1]
    seq = n // batch
    consts = [wa, wb, wo, fg, wup, cw, cb, wdn, ng]
    const_specs = [_const_spec(a.shape) for a in consts]
    if history is None:
        tm = min(FFN_ROWS, seq)
        nt = seq // tm
        grid = (batch, nt)
        row = lambda b, j: (b * nt + j, 0)
        extra, extra_specs = [], []
        conv_shape = [jax.ShapeDtypeStruct((batch, 8, two_ff), F32)]
        conv_spec = [pl.BlockSpec((None, 8, two_ff), lambda b, j: (b, 0, 0))]
        scratch = [pltpu.VMEM((tm, d_model), F32), pltpu.VMEM((8, two_ff), F32)]
        seq_rows = None
    else:
        tm = n
        grid = (1,)
        row = lambda i: (0, 0)
        extra = list(history)
        extra_specs = [_const_spec(h.shape) for h in history]
        conv_shape = [jax.ShapeDtypeStruct((batch, two_ff), F32)] * 2
        conv_spec = [pl.BlockSpec((batch, two_ff), row)] * 2
        scratch = [pltpu.VMEM((tm, d_model), F32), pltpu.VMEM((tm, LANES), F32)]
        seq_rows = seq
    tile = lambda cols: pl.BlockSpec((tm, cols), row)
    return pl.pallas_call(
        functools.partial(_merge_ffn_body, d_ff=d_ff, seq_rows=seq_rows),
        grid=grid,
        in_specs=[tile(d_model), tile(width), tile(width), tile(d_model), tile(d_model)]
                 + const_specs + extra_specs,
        out_specs=[tile(d_model)] + conv_spec,
        out_shape=[jax.ShapeDtypeStruct((n, d_model), F32)] + conv_shape,
        scratch_shapes=scratch,
        compiler_params=_params(len(grid)),
        name="merge_ffn" if history is None else "merge_ffn_streams",
    )(x2d, oa, ob, ga, gb, *consts, *extra)


def kernel(x_prompt, x_sample, cache_diff_k, cache_diff_v, cache_sb_k, cache_sb_v, state_conv,
           attn_norm_g, w_in, lambda_q1, lambda_k1, lambda_q2, lambda_k2, subln_g, w_branch_a,
           w_branch_b, w_out, ffn_norm_g, w_up, conv_w, conv_b, w_down, final_norm_g):
    depth = w_in.shape[0]
    assert depth == 1, "single-layer trunk only"
    batch, seq, d_model = x_prompt.shape
    dec_batch, dec_seq, _ = x_sample.shape
    past = cache_diff_k.shape[2]
    width = d_model // 2
    da_heads, _, da_dim = cache_diff_k.shape[3:]
    da_v = cache_diff_v.shape[-1]
    sb_heads, sb_dim = cache_sb_k.shape[3:]
    assert da_dim == sb_dim and 2 * da_dim == LANES and da_v == LANES
    half = da_dim

    bf = lambda w: w[0].astype(BF16)
    row1 = lambda v: v.reshape(1, -1)
    w_in_bf = bf(w_in)
    consts = (bf(w_branch_a), bf(w_branch_b), bf(w_out), ffn_norm_g, bf(w_up), conv_w[0], conv_b,
              bf(w_down), row1(final_norm_g))
    lams = [row1(v[0]) for v in (lambda_q1, lambda_k1, lambda_q2, lambda_k2)]
    g_row, g_col = row1(subln_g[0]), subln_g[0].reshape(-1, 1)
    specs = lambda arrays: [_const_spec(a.shape) for a in arrays]

    xp = x_prompt.reshape(batch * seq, d_model)
    cos_p, sin_p = _rope_tables(jnp.arange(seq), da_dim)
    (p_dk, p_dv, p_sk, p_sv, qd, kd, vd, qs, ks, vs, ga, gb) = _in_proj(
        xp, attn_norm_g, w_in_bf, cos_p, sin_p, head_dim=da_dim)
    oa = _prompt_attention(functools.partial(_diff_prompt_body, rows=ATTN_ROWS, half=half), "diff_prompt",
                           lams + [g_col], specs(lams + [g_col]), qd, kd, vd, batch=batch, seq=seq)
    ob = _prompt_attention(functools.partial(_stick_prompt_body, rows=ATTN_ROWS, half=half), "stick_prompt",
                           [], [], qs, ks, vs, batch=batch, seq=seq)
    y_p, conv_p = _merge_ffn(xp, oa, ob, ga, gb, *consts, batch=batch)

    xs = x_sample.reshape(dec_batch * dec_seq, d_model)
    cos_s, sin_s = _rope_tables(past + jnp.arange(dec_seq), da_dim)
    cos_s, sin_s = jnp.tile(cos_s, (dec_batch, 1)), jnp.tile(sin_s, (dec_batch, 1))
    (s_dk, s_dv, s_sk, s_sv, qd, kd, vd, qs, ks, vs, ga, gb) = _in_proj(
        xs, attn_norm_g, w_in_bf, cos_s, sin_s, head_dim=da_dim)
    minor = lambda c: jnp.moveaxis(c[0], 1, -1).reshape(dec_batch, width, past)
    oa = _decode_attention(functools.partial(_diff_decode_body, rows=dec_seq, half=half), "diff_decode",
                           lams + [g_row], specs(lams + [g_row]), qd, kd, vd,
                           minor(cache_diff_k), cache_diff_v[0], rows=dec_seq)
    ob = _decode_attention(functools.partial(_stick_decode_body, rows=dec_seq, half=half), "stick_decode",
                           [], [], qs, ks, vs, minor(cache_sb_k), minor(cache_sb_v), rows=dec_seq)
    y_s, last2, last1 = _merge_ffn(xs, oa, ob, ga, gb, *consts, batch=dec_batch,
                                   history=(state_conv[0, :, 0], state_conv[0, :, 1]))

    conv_p = conv_p[:, 6:]
    conv_s = jnp.stack([last2, last1], axis=1)
    return (y_p.reshape(batch, seq, d_model), y_s.reshape(dec_batch, dec_seq, d_model),
            p_dk.reshape(depth, batch, seq, da_heads, 2, da_dim),
            p_dv.reshape(depth, batch, seq, da_heads, da_v),
            p_sk.reshape(depth, batch, seq, sb_heads, sb_dim),
            p_sv.reshape(depth, batch, seq, sb_heads, sb_dim),
            conv_p[None],
            s_dk.reshape(depth, dec_batch, dec_seq, da_heads, 2, da_dim),
            s_dv.reshape(depth, dec_batch, dec_seq, da_heads, da_v),
            s_sk.reshape(depth, dec_batch, dec_seq, sb_heads, sb_dim),
            s_sv.reshape(depth, dec_batch, dec_seq, sb_heads, sb_dim),
            conv_s[None])
```

```python
import functools

import jax
import jax.numpy as jnp
from jax import lax
from jax.experimental import pallas as pl
from jax.experimental.pallas import tpu as pltpu

F32 = jnp.float32
BF16 = jnp.bfloat16

CHUNK = 64
ROPE_THETA = 10000.0
NORM_EPS = 1e-6
NEG_INF = -1e30
LAMBDA_INIT_LAYER0 = 0.8 - 0.6 * 1.0
LOG2_E = 1.4426950408889634
DEAD_SOFTPLUS_SUM = 104.0

LANES = 128
VMEM_LIMIT_BYTES = 56 * 1024 * 1024

PROJ_ROWS = 512
ATTN_ROWS = 256
FFN_ROWS = 512
FFN_COLS = 256
DECODE_KEYS = 256


def _rms(x, g):
    return x * lax.rsqrt(jnp.mean(x * x, axis=-1, keepdims=True) + NORM_EPS) * g


def _dot(a, b):
    return jnp.dot(a, b, preferred_element_type=F32)


def _dot_nt(a, b):
    return lax.dot_general(a, b, (((1,), (1,)), ((), ())), preferred_element_type=F32)


def _log2(n):
    assert n > 0 and n & (n - 1) == 0, n
    return n.bit_length() - 1


def _mod_pow2(x, n):
    _log2(n)
    return jnp.bitwise_and(x, n - 1)


def _div_pow2(x, n):
    return jnp.right_shift(x, _log2(n))


def _lanes(g):
    return slice(g * LANES, (g + 1) * LANES)


def _const_spec(shape):
    return pl.BlockSpec(shape, lambda *_: (0,) * len(shape), pipeline_mode=pl.Buffered(1))


def _params(n_axes):
    return pltpu.CompilerParams(dimension_semantics=("arbitrary",) * n_axes,
                                vmem_limit_bytes=VMEM_LIMIT_BYTES)


def _in_proj_body(x_ref, g_ref, w_ref, cos_ref, sin_ref,
                  kd_ref, vd_ref, ks_ref, vs_ref,
                  qd_bf, kd_bf, vd_bf, qs_bf, ks_bf, vs_bf, ga_ref, gb_ref,
                  *, width, d_model, head_dim, q_scale, position_minor):
    xb = _rms(x_ref[...], g_ref[...]).astype(BF16)

    def proj(c0, n):
        return _dot(xb, w_ref[:, c0:c0 + n])

    def put(ref, c, z):
        if position_minor:
            ref[_lanes(c), :] = z.T
        else:
            ref[:, _lanes(c)] = z

    cos = cos_ref[...]
    sin = sin_ref[...]
    lane = lax.broadcasted_iota(jnp.int32, cos.shape, 1)
    half = head_dim // 2
    first_half = _mod_pow2(lane, head_dim) < half

    def rope(z):
        rot = jnp.where(first_half, pltpu.roll(z, LANES - half, 1), pltpu.roll(z, half, 1))
        return z * cos + rot * sin

    tiles = range(width // LANES)

    def diff_q(z):
        for c in tiles:
            qd_bf[:, _lanes(c)] = (rope(z[:, _lanes(c)]) * q_scale).astype(BF16)

    def diff_k(z):
        for c in tiles:
            rk = rope(z[:, _lanes(c)])
            put(kd_ref, c, rk)
            kd_bf[:, _lanes(c)] = rk.astype(BF16)

    def diff_v(z):
        vd_bf[...] = z.astype(BF16)
        if position_minor:
            for h in tiles:
                vd_ref[pl.ds(h, z.shape[0], stride=len(tiles)), :] = z[:, _lanes(h)]
        else:
            vd_ref[...] = z

    def stick_q(z):
        qs_bf[...] = (z * q_scale).astype(BF16)

    def stick_kv(out_ref, bf_ref):
        def post(z):
            bf_ref[...] = z.astype(BF16)
            for c in tiles:
                put(out_ref, c, z[:, _lanes(c)])
        return post

    def gate(out_ref, c):
        def post(z):
            out_ref[:, c * width:(c + 1) * width] = jax.nn.sigmoid(z).astype(BF16)
        return post

    posts = [diff_q, diff_k, diff_v, stick_q, stick_kv(ks_ref, ks_bf), stick_kv(vs_ref, vs_bf)]
    posts += [gate(ref, c) for ref in (ga_ref, gb_ref) for c in range(d_model // width)]
    z = {}

    def matmul(u):
        z[u] = proj(u * width, width)

    _staggered([matmul, lambda u: posts[u](z.pop(u))], len(posts))


def _in_proj(x2d, g, w_bf, cos_t, sin_t, *, head_dim, seq=None):
    n, d_model = x2d.shape
    n_in = w_bf.shape[1]
    width = d_model // 2
    tm = min(PROJ_ROWS, n)
    n_tab = cos_t.shape[0] // tm
    row = lambda i: (i, 0)
    tab = lambda i: (i % n_tab, 0)
    row_spec = pl.BlockSpec((tm, width), row)
    bf_out = jax.ShapeDtypeStruct((n, width), BF16)
    gate_out = jax.ShapeDtypeStruct((n, d_model), BF16)
    if seq is None:
        cache_specs = [row_spec] * 4
        cache_shapes = [jax.ShapeDtypeStruct((n, width), F32)] * 4
    else:
        nt = seq // tm
        minor_spec = pl.BlockSpec((None, width, tm), lambda i: (i // nt, 0, i % nt))
        minor_shape = jax.ShapeDtypeStruct((n // seq, width, seq), F32)
        heads = width // LANES
        cache_specs = [minor_spec, pl.BlockSpec((tm * heads, LANES), row), minor_spec, minor_spec]
        cache_shapes = [minor_shape, jax.ShapeDtypeStruct((n * heads, LANES), F32), minor_shape, minor_shape]
    body = functools.partial(_in_proj_body, width=width, d_model=d_model, head_dim=head_dim,
                             q_scale=head_dim ** -0.5, position_minor=seq is not None)
    return pl.pallas_call(
        body,
        grid=(n // tm,),
        in_specs=[pl.BlockSpec((tm, d_model), row),
                  _const_spec((1, d_model)),
                  _const_spec((d_model, n_in)),
                  pl.BlockSpec((tm, LANES), tab),
                  pl.BlockSpec((tm, LANES), tab)],
        out_specs=cache_specs + [row_spec] * 6 + [pl.BlockSpec((tm, d_model), row)] * 2,
        out_shape=cache_shapes + [bf_out] * 6 + [gate_out] * 2,
        compiler_params=_params(1),
        name="in_proj",
    )(x2d, g, w_bf, cos_t, sin_t)


def _rope_tables(pos, head_dim):
    half = head_dim // 2
    inv = ROPE_THETA ** (-jnp.arange(half, dtype=F32) * 2.0 / head_dim)
    ang = pos.astype(F32)[:, None] * inv[None, :]
    cos = jnp.cos(ang)
    sin = jnp.sin(ang)
    reps = LANES // head_dim
    return (jnp.tile(jnp.concatenate([cos, cos], -1), (1, reps)),
            jnp.tile(jnp.concatenate([-sin, sin], -1), (1, reps)))


def _stack_halves(q, half):
    lane = lax.broadcasted_iota(jnp.int32, q.shape, 1)
    zero = jnp.zeros_like(q)
    return jnp.concatenate([jnp.where(lane < half, q, zero), jnp.where(lane >= half, q, zero)], axis=0)


def _exp(x, sign=1.0):
    return jnp.exp2(x * (sign * LOG2_E))


def _softplus_parts(z):
    sp = jnp.maximum(z, 0.0) + jnp.log(1.0 + _exp(jnp.abs(z), -1.0))
    return sp, z - sp


def _split_bf16(x):
    hi = x.astype(BF16)
    return hi, (x - hi.astype(F32)).astype(BF16)


def _suffix_matrix(n, keys_on_rows):
    a = lax.broadcasted_iota(jnp.int32, (n, n), 0)
    b = lax.broadcasted_iota(jnp.int32, (n, n), 1)
    after = (b > a) if keys_on_rows else (a > b)
    return jnp.where(after, 1.0, 0.0).astype(BF16)


def _lambda_value(lq1, lk1, lq2, lk2):
    return (jnp.exp(jnp.sum(lq1[...] * lk1[...], axis=-1, keepdims=True))
            - jnp.exp(jnp.sum(lq2[...] * lk2[...], axis=-1, keepdims=True))
            + LAMBDA_INIT_LAYER0)


def _staggered(stages, n):
    order = [0] + list(range(len(stages) - 1, 0, -1))
    for t in range(n + len(stages) - 1):
        for s in order:
            if 0 <= t - s < n:
                stages[s](t - s)


def _softmax_update_t(s, vt, m, l, acc, mask):
    if mask is not None:
        s = jnp.where(mask, s, NEG_INF)
    m_new = jnp.maximum(m, jnp.max(s, axis=0, keepdims=True))
    alpha = _exp(m - m_new)
    p = _exp(s - m_new)
    l_new = alpha * l + jnp.sum(p, axis=0, keepdims=True)
    acc_new = alpha * acc + _dot(vt, p.astype(BF16))
    return m_new, l_new, acc_new


def _stick_later_t(z, suffix, mask):
    sp, log_sig = _softplus_parts(z)
    if mask is not None:
        sp = jnp.where(mask, sp, 0.0)
    hi, lo = _split_bf16(sp)
    return log_sig, sp, _dot(suffix, hi) + _dot(suffix, lo)


def _stick_update_t(log_sig, sp, later, vt, r, acc, mask):
    a = _exp(log_sig - later - r)
    if mask is not None:
        a = jnp.where(mask, a, 0.0)
    return r + jnp.sum(sp, axis=0, keepdims=True), acc + _dot(vt, a.astype(BF16))


def _key_query_positions(n_keys, rows):
    s = lax.broadcasted_iota(jnp.int32, (n_keys, 2 * rows), 0)
    t = _mod_pow2(lax.broadcasted_iota(jnp.int32, (n_keys, 2 * rows), 1), rows)
    return s, t


def _diff_finish_t(acc, l, lam, g_col, rows):
    o = acc / l
    o = o[:, :rows] - lam * o[:, rows:]
    y = o * lax.rsqrt(jnp.mean(o * o, axis=0, keepdims=True) + NORM_EPS) * g_col
    return (y * (1.0 - LAMBDA_INIT_LAYER0)).T.astype(BF16)


def _stick_finish_t(acc, rows, half):
    row = lax.broadcasted_iota(jnp.int32, (LANES, rows), 0)
    return jnp.where(row < half, acc[:, :rows], acc[:, rows:]).T.astype(BF16)


def _transpose_values(v_ref, vt_ref):
    nblk, _, n = vt_ref.shape
    for c in range(nblk):
        vt_ref[c] = v_ref[c * n:(c + 1) * n, :].astype(F32).T.astype(BF16)


def _diff_prompt_body(lq1, lk1, lq2, lk2, g_ref, q_ref, k_ref, v_ref, o_ref, vt_ref, acc_ref, *, rows, half):
    i = pl.program_id(1)
    groups = q_ref.shape[1] // LANES

    @pl.when(i == 0)
    def _():
        _transpose_values(v_ref, vt_ref)

    qs = [_stack_halves(q_ref[:, _lanes(g)], half) for g in range(groups)]
    acc_ref[...] = jnp.zeros_like(acc_ref)

    def step(j, stats, mask):
        off = pl.multiple_of(j * rows, rows)
        scores, out = {}, {}

        def score(g):
            scores[g] = _dot_nt(k_ref[pl.ds(off, rows), _lanes(g)], qs[g])

        def update(g):
            m, l, acc = _softmax_update_t(scores.pop(g), vt_ref[j, _lanes(g), :], *stats[g], acc_ref[g], mask)
            acc_ref[g] = acc
            out[g] = (m, l)

        _staggered([score, update], groups)
        return tuple(out[g] for g in range(groups))

    init = tuple((jnp.full((1, 2 * rows), NEG_INF, F32), jnp.zeros((1, 2 * rows), F32))
                 for _ in range(groups))
    stats = lax.fori_loop(0, i, lambda j, st: step(j, st, None), init)
    s, t = _key_query_positions(rows, rows)
    stats = step(i, stats, _div_pow2(s, CHUNK) <= _div_pow2(t, CHUNK))
    lam = _lambda_value(lq1, lk1, lq2, lk2)
    for g in range(groups):
        o_ref[:, _lanes(g)] = _diff_finish_t(acc_ref[g], stats[g][1], lam, g_ref[...], rows)


def _stick_prompt_body(q_ref, k_ref, v_ref, o_ref, vt_ref, acc_ref, *, rows, half):
    i = pl.program_id(1)
    groups = q_ref.shape[1] // LANES

    @pl.when(i == 0)
    def _():
        _transpose_values(v_ref, vt_ref)

    qs = [_stack_halves(q_ref[:, _lanes(g)], half) for g in range(groups)]
    suffix = _suffix_matrix(rows, keys_on_rows=True)
    acc_ref[...] = jnp.zeros_like(acc_ref)

    def step(j, carry, mask):
        off = pl.multiple_of(j * rows, rows)
        logits, parts, out = {}, {}, {}

        def logit(g):
            logits[g] = _dot_nt(k_ref[pl.ds(off, rows), _lanes(g)], qs[g])

        def later(g):
            parts[g] = _stick_later_t(logits.pop(g), suffix, mask)

        def update(g):
            out[g], acc = _stick_update_t(*parts.pop(g), vt_ref[j, _lanes(g), :], carry[g], acc_ref[g], mask)
            acc_ref[g] = acc

        _staggered([logit, later, update], groups)
        return tuple(out[g] for g in range(groups))

    def live(carry):
        return jnp.min(functools.reduce(jnp.minimum, carry)) < DEAD_SOFTPLUS_SUM

    def older_block(state):
        jj, _, carry = state
        carry = step(i - 1 - jj, carry, None)
        return jj + 1, live(carry), carry

    s, t = _key_query_positions(rows, rows)
    carry = step(i, tuple(jnp.zeros((1, 2 * rows), F32) for _ in range(groups)), s < t)
    lax.while_loop(lambda state: (state[0] < i) & state[1], older_block, (jnp.int32(0), live(carry), carry))
    for g in range(groups):
        o_ref[:, _lanes(g)] = _stick_finish_t(acc_ref[g], rows, half)


def _prompt_attention(body, name, extra_inputs, extra_specs, q, k, v, *, batch, seq):
    n, width = q.shape
    rows = ATTN_ROWS
    nq = seq // rows
    q_spec = pl.BlockSpec((rows, width), lambda b, i: (b * nq + i, 0))
    kv_spec = pl.BlockSpec((seq, width), lambda b, i: (b, 0))
    return pl.pallas_call(
        body,
        grid=(batch, nq),
        in_specs=extra_specs + [q_spec, kv_spec, kv_spec],
        out_specs=q_spec,
        out_shape=jax.ShapeDtypeStruct((n, width), BF16),
        scratch_shapes=[pltpu.VMEM((nq, width, rows), BF16),
                        pltpu.VMEM((width // LANES, LANES, 2 * rows), F32)],
        compiler_params=_params(2),
        name=name,
    )(*extra_inputs, q, k, v)


def _pad_rows(x, n):
    return jnp.concatenate([x, jnp.zeros((n - x.shape[0], x.shape[1]), x.dtype)], axis=0)


def _stacked_positions(rows, n_keys):
    t = _mod_pow2(lax.broadcasted_iota(jnp.int32, (2 * rows, n_keys), 0), rows)
    s = lax.broadcasted_iota(jnp.int32, (2 * rows, n_keys), 1)
    return t, s


def _diff_decode_body(lq1, lk1, lq2, lk2, g_ref, q_ref, kn_ref, vn_ref, ckt_ref, cv_ref, o_ref,
                      *, rows, half):
    past = ckt_ref.shape[1]
    lam = _lambda_value(lq1, lk1, lq2, lk2)
    t, s = _stacked_positions(rows, LANES)
    new_mask = (s < rows) & (_div_pow2(past + s, CHUNK) <= _div_pow2(past + t, CHUNK))
    groups = q_ref.shape[1] // LANES
    for g in range(groups):
        sl = _lanes(g)
        qs = _stack_halves(q_ref[:, sl], half)
        s_c = _dot(qs, ckt_ref[sl, :].astype(BF16))
        s_n = jnp.where(new_mask, _dot_nt(qs, _pad_rows(kn_ref[:, sl], LANES)), NEG_INF)
        m = jnp.maximum(jnp.max(s_c, axis=1, keepdims=True), jnp.max(s_n, axis=1, keepdims=True))
        p_c = _exp(s_c - m)
        p_n = _exp(s_n - m)
        l = jnp.sum(p_c, axis=1, keepdims=True) + jnp.sum(p_n, axis=1, keepdims=True)
        v_c = cv_ref[pl.ds(g, past, stride=groups), :]
        acc = (_dot(p_c.astype(BF16), v_c.astype(BF16))
               + _dot(p_n.astype(BF16), _pad_rows(vn_ref[:, sl], LANES)))
        o = acc / l
        o = o[:rows] - lam * o[rows:]
        o_ref[:, sl] = (_rms(o, g_ref[...]) * (1.0 - LAMBDA_INIT_LAYER0)).astype(BF16)


def _stick_decode_body(q_ref, kn_ref, vn_ref, ckt_ref, cvt_ref, o_ref, *, rows, half):
    past = ckt_ref.shape[1]
    suffix = _suffix_matrix(DECODE_KEYS, keys_on_rows=False)
    suffix_new = _suffix_matrix(LANES, keys_on_rows=False)
    t, s = _stacked_positions(rows, LANES)
    new_mask = (s < rows) & (s < t)
    lane = lax.broadcasted_iota(jnp.int32, (rows, LANES), 1)
    for g in range(q_ref.shape[1] // LANES):
        sl = _lanes(g)
        qs = _stack_halves(q_ref[:, sl], half)
        sp, log_sig = _softplus_parts(_dot_nt(qs, _pad_rows(kn_ref[:, sl], LANES)))
        sp = jnp.where(new_mask, sp, 0.0)
        hi, lo = _split_bf16(sp)
        a = jnp.where(new_mask, _exp(log_sig - _dot(hi, suffix_new) - _dot(lo, suffix_new)), 0.0)
        acc = _dot(a.astype(BF16), _pad_rows(vn_ref[:, sl], LANES))
        carry = jnp.sum(sp, axis=1, keepdims=True)
        sp, log_sig = _softplus_parts(_dot(qs, ckt_ref[sl, :].astype(BF16)))
        hi, lo = _split_bf16(sp)
        vt = cvt_ref[sl, :].astype(BF16)
        for blk in reversed(range(past // DECODE_KEYS)):
            cols = slice(blk * DECODE_KEYS, (blk + 1) * DECODE_KEYS)
            later = _dot(hi[:, cols], suffix) + _dot(lo[:, cols], suffix)
            a = _exp(log_sig[:, cols] - later - carry)
            acc = acc + _dot_nt(a.astype(BF16), vt[:, cols])
            carry = carry + jnp.sum(sp[:, cols], axis=1, keepdims=True)
        o_ref[:, sl] = jnp.where(lane < half, acc[:rows], acc[rows:]).astype(BF16)


def _decode_attention(body, name, extra_inputs, extra_specs, q, k_new, v_new, cache_kt, cache_v, *, rows):
    n, width = q.shape
    new_spec = pl.BlockSpec((rows, width), lambda b: (b, 0))
    cache_spec = lambda c: pl.BlockSpec((None,) + c.shape[1:], lambda b: (b,) + (0,) * (c.ndim - 1))
    return pl.pallas_call(
        body,
        grid=(n // rows,),
        in_specs=extra_specs + [new_spec, new_spec, new_spec, cache_spec(cache_kt), cache_spec(cache_v)],
        out_specs=new_spec,
        out_shape=jax.ShapeDtypeStruct((n, width), BF16),
        compiler_params=_params(1),
        name=name,
    )(*extra_inputs, q, k_new, v_new, cache_kt, cache_v)


def _merge_ffn_body(*refs, d_ff, seq_rows):
    (x_ref, oa_ref, ob_ref, ga_ref, gb_ref, wa_ref, wb_ref, wo_ref, fg_ref, wup_ref, cw_ref, cb_ref,
     wdn_ref, ng_ref) = refs[:14]
    if seq_rows is None:
        y_ref, conv_ref, acc_ref, carry_ref = refs[14:]
    else:
        h0_ref, h1_ref, y_ref, last2_ref, last1_ref, acc_ref, u_ref = refs[14:]
    tm = x_ref.shape[0]
    assert cw_ref.shape[0] == 3

    mix = (ga_ref[...].astype(F32) * _dot(oa_ref[...], wa_ref[...])
           + gb_ref[...].astype(F32) * _dot(ob_ref[...], wb_ref[...]))
    x1 = x_ref[...] + _dot(mix.astype(BF16), wo_ref[...])
    xn = _rms(x1, fg_ref[...]).astype(BF16)

    if seq_rows is None:
        @pl.when(pl.program_id(1) == 0)
        def _():
            carry_ref[...] = jnp.zeros_like(carry_ref)
        pos = lax.broadcasted_iota(jnp.int32, (tm, FFN_COLS), 0)
    else:
        pos = _mod_pow2(lax.broadcasted_iota(jnp.int32, (tm, FFN_COLS), 0), seq_rows)

    def conv(u, cols):
        if seq_rows is None:
            prev2 = carry_ref[6:7, cols]
            prev1 = carry_ref[7:8, cols]
            carry_ref[:, cols] = u[tm - 8:, :]
            conv_ref[:, cols] = u[tm - 8:, :]
        else:
            n_seq = tm // seq_rows
            expand = lambda h: jnp.broadcast_to(h[:, None, :], (n_seq, seq_rows, FFN_COLS)).reshape(tm, FFN_COLS)
            prev2 = expand(h0_ref[:, cols])
            prev1 = expand(h1_ref[:, cols])
            for part in range(FFN_COLS // LANES):
                lanes = _lanes(part)
                out = slice(cols.start + part * LANES, cols.start + (part + 1) * LANES)
                u_ref[...] = u[:, lanes]
                last2_ref[:, out] = u_ref[pl.ds(seq_rows - 2, n_seq, stride=seq_rows), :]
                last1_ref[:, out] = u_ref[pl.ds(seq_rows - 1, n_seq, stride=seq_rows), :]
        u1 = jnp.where(pos == 0, prev1, pltpu.roll(u, 1, 0))
        u2 = jnp.where(pos == 0, prev2, jnp.where(pos == 1, prev1, pltpu.roll(u, 2, 0)))
        return cb_ref[:, cols] + cw_ref[0:1, cols] * u2 + cw_ref[1:2, cols] * u1 + cw_ref[2:3, cols] * u

    gate_cols = lambda c: slice(c * FFN_COLS, (c + 1) * FFN_COLS)
    val_cols = lambda c: slice(d_ff + c * FFN_COLS, d_ff + (c + 1) * FFN_COLS)
    ups = {}

    def up(c):
        ups[c] = _dot(xn, wup_ref[:, gate_cols(c)]), _dot(xn, wup_ref[:, val_cols(c)])

    def down(c):
        ug, uv = ups.pop(c)
        gate = conv(ug, gate_cols(c))
        val = conv(uv, val_cols(c))
        h = (gate * jax.nn.sigmoid(gate) * val).astype(BF16)
        part = _dot(h, wdn_ref[gate_cols(c), :])
        if c == 0:
            acc_ref[...] = part
        else:
            acc_ref[...] += part

    _staggered([up, down], d_ff // FFN_COLS)
    y_ref[...] = _rms(x1 + acc_ref[...], ng_ref[...])


def _merge_ffn(x2d, oa, ob, ga, gb, wa, wb, wo, fg, wup, cw, cb, wdn, ng, *, batch, history=None):
    n, d_model = x2d.shape
    width = oa.shape[1]
    d_ff = wdn.shape[0]
    two_ff = wup.shape[1]
    seq = n // batch
    consts = [wa, wb, wo, fg, wup, cw, cb, wdn, ng]
    const_specs = [_const_spec(a.shape) for a in consts]
    if history is None:
        tm = min(FFN_ROWS, seq)
        nt = seq // tm
        grid = (batch, nt)
        row = lambda b, j: (b * nt + j, 0)
        extra, extra_specs = [], []
        conv_shape = [jax.ShapeDtypeStruct((batch, 8, two_ff), F32)]
        conv_spec = [pl.BlockSpec((None, 8, two_ff), lambda b, j: (b, 0, 0))]
        scratch = [pltpu.VMEM((tm, d_model), F32), pltpu.VMEM((8, two_ff), F32)]
        seq_rows = None
    else:
        tm = n
        grid = (1,)
        row = lambda i: (0, 0)
        extra = list(history)
        extra_specs = [_const_spec(h.shape) for h in history]
        conv_shape = [jax.ShapeDtypeStruct((batch, two_ff), F32)] * 2
        conv_spec = [pl.BlockSpec((batch, two_ff), row)] * 2
        scratch = [pltpu.VMEM((tm, d_model), F32), pltpu.VMEM((tm, LANES), F32)]
        seq_rows = seq
    tile = lambda cols: pl.BlockSpec((tm, cols), row)
    return pl.pallas_call(
        functools.partial(_merge_ffn_body, d_ff=d_ff, seq_rows=seq_rows),
        grid=grid,
        in_specs=[tile(d_model), tile(width), tile(width), tile(d_model), tile(d_model)]
                 + const_specs + extra_specs,
        out_specs=[tile(d_model)] + conv_spec,
        out_shape=[jax.ShapeDtypeStruct((n, d_model), F32)] + conv_shape,
        scratch_shapes=scratch,
        compiler_params=_params(len(grid)),
        name="merge_ffn" if history is None else "merge_ffn_streams",
    )(x2d, oa, ob, ga, gb, *consts, *extra)


def kernel(x_prompt, x_sample, cache_diff_k, cache_diff_v, cache_sb_k, cache_sb_v, state_conv,
           attn_norm_g, w_in, lambda_q1, lambda_k1, lambda_q2, lambda_k2, subln_g, w_branch_a,
           w_branch_b, w_out, ffn_norm_g, w_up, conv_w, conv_b, w_down, final_norm_g):
    depth = w_in.shape[0]
    assert depth == 1, "single-layer trunk only"
    batch, seq, d_model = x_prompt.shape
    dec_batch, dec_seq, _ = x_sample.shape
    past = cache_diff_k.shape[2]
    width = d_model // 2
    da_heads, _, da_dim = cache_diff_k.shape[3:]
    da_v = cache_diff_v.shape[-1]
    sb_heads, sb_dim = cache_sb_k.shape[3:]
    assert da_dim == sb_dim and 2 * da_dim == LANES and da_v == LANES
    half = da_dim

    bf = lambda w: w[0].astype(BF16)
    row1 = lambda v: v.reshape(1, -1)
    w_in_bf = bf(w_in)
    consts = (bf(w_branch_a), bf(w_branch_b), bf(w_out), ffn_norm_g, bf(w_up), conv_w[0], conv_b,
              bf(w_down), row1(final_norm_g))
    lams = [row1(v[0]) for v in (lambda_q1, lambda_k1, lambda_q2, lambda_k2)]
    g_row, g_col = row1(subln_g[0]), subln_g[0].reshape(-1, 1)
    specs = lambda arrays: [_const_spec(a.shape) for a in arrays]

    xp = x_prompt.reshape(batch * seq, d_model)
    cos_p, sin_p = _rope_tables(jnp.arange(seq), da_dim)
    (p_dk, p_dv, p_sk, p_sv, qd, kd, vd, qs, ks, vs, ga, gb) = _in_proj(
        xp, attn_norm_g, w_in_bf, cos_p, sin_p, head_dim=da_dim, seq=seq)
    major = lambda c, *dims: jnp.moveaxis(c.reshape(batch, *dims, seq), -1, 1)
    p_dk, p_sk, p_sv = (major(p_dk, da_heads, 2, da_dim), major(p_sk, sb_heads, sb_dim),
                        major(p_sv, sb_heads, sb_dim))
    oa = _prompt_attention(functools.partial(_diff_prompt_body, rows=ATTN_ROWS, half=half), "diff_prompt",
                           lams + [g_col], specs(lams + [g_col]), qd, kd, vd, batch=batch, seq=seq)
    ob = _prompt_attention(functools.partial(_stick_prompt_body, rows=ATTN_ROWS, half=half), "stick_prompt",
                           [], [], qs, ks, vs, batch=batch, seq=seq)
    y_p, conv_p = _merge_ffn(xp, oa, ob, ga, gb, *consts, batch=batch)

    xs = x_sample.reshape(dec_batch * dec_seq, d_model)
    cos_s, sin_s = _rope_tables(past + jnp.arange(dec_seq), da_dim)
    cos_s, sin_s = jnp.tile(cos_s, (dec_batch, 1)), jnp.tile(sin_s, (dec_batch, 1))
    (s_dk, s_dv, s_sk, s_sv, qd, kd, vd, qs, ks, vs, ga, gb) = _in_proj(
        xs, attn_norm_g, w_in_bf, cos_s, sin_s, head_dim=da_dim)
    minor = lambda c: jnp.moveaxis(c[0], 1, -1).reshape(dec_batch, width, past)
    oa = _decode_attention(functools.partial(_diff_decode_body, rows=dec_seq, half=half), "diff_decode",
                           lams + [g_row], specs(lams + [g_row]), qd, kd, vd,
                           minor(cache_diff_k), cache_diff_v[0].reshape(dec_batch, past * da_heads, da_v),
                           rows=dec_seq)
    ob = _decode_attention(functools.partial(_stick_decode_body, rows=dec_seq, half=half), "stick_decode",
                           [], [], qs, ks, vs, minor(cache_sb_k), minor(cache_sb_v), rows=dec_seq)
    y_s, last2, last1 = _merge_ffn(xs, oa, ob, ga, gb, *consts, batch=dec_batch,
                                   history=(state_conv[0, :, 0], state_conv[0, :, 1]))

    conv_p = conv_p[:, 6:]
    conv_s = jnp.stack([last2, last1], axis=1)
    return (y_p.reshape(batch, seq, d_model), y_s.reshape(dec_batch, dec_seq, d_model),
            p_dk[None], p_dv.reshape(depth, batch, seq, da_heads, da_v), p_sk[None], p_sv[None],
            conv_p[None],
            s_dk.reshape(depth, dec_batch, dec_seq, da_heads, 2, da_dim),
            s_dv.reshape(depth, dec_batch, dec_seq, da_heads, da_v),
            s_sk.reshape(depth, dec_batch, dec_seq, sb_heads, sb_dim),
            s_sv.reshape(depth, dec_batch, dec_seq, sb_heads, sb_dim),
            conv_s[None])
```

```python
import functools

import jax
import jax.numpy as jnp
from jax import lax
from jax.experimental import pallas as pl
from jax.experimental.pallas import tpu as pltpu

F32 = jnp.float32
BF16 = jnp.bfloat16

CHUNK = 64
ROPE_THETA = 10000.0
NORM_EPS = 1e-6
NEG_INF = -1e30
LAMBDA_INIT_LAYER0 = 0.8 - 0.6 * 1.0
LOG2_E = 1.4426950408889634
DEAD_SOFTPLUS_SUM = 104.0

LANES = 128
VMEM_LIMIT_BYTES = 56 * 1024 * 1024

PROJ_ROWS = 512
ATTN_ROWS = 256
FFN_ROWS = 512
FFN_COLS = 256
DECODE_KEYS = 256


def _rms(x, g):
    return x * lax.rsqrt(jnp.mean(x * x, axis=-1, keepdims=True) + NORM_EPS) * g


def _dot(a, b):
    return jnp.dot(a, b, preferred_element_type=F32)


def _dot_nt(a, b):
    return lax.dot_general(a, b, (((1,), (1,)), ((), ())), preferred_element_type=F32)


def _log2(n):
    assert n > 0 and n & (n - 1) == 0, n
    return n.bit_length() - 1


def _mod_pow2(x, n):
    _log2(n)
    return jnp.bitwise_and(x, n - 1)


def _div_pow2(x, n):
    return jnp.right_shift(x, _log2(n))


def _lanes(g):
    return slice(g * LANES, (g + 1) * LANES)


def _const_spec(shape):
    return pl.BlockSpec(shape, lambda *_: (0,) * len(shape), pipeline_mode=pl.Buffered(1))


def _params(n_axes, **flags):
    return pltpu.CompilerParams(dimension_semantics=("arbitrary",) * n_axes,
                                vmem_limit_bytes=VMEM_LIMIT_BYTES, flags=flags or None)


def _in_proj_body(x_ref, g_ref, w_ref, cos_ref, sin_ref,
                  kd_ref, vd_ref, ks_ref, vs_ref,
                  qd_bf, kd_bf, vd_bf, qs_bf, ks_bf, vs_bf, ga_ref, gb_ref,
                  *, width, d_model, head_dim, q_scale, position_minor):
    xb = _rms(x_ref[...], g_ref[...]).astype(BF16)

    def proj(c0, n):
        return _dot(xb, w_ref[:, c0:c0 + n])

    def put(ref, c, z):
        if position_minor:
            ref[_lanes(c), :] = z.T
        else:
            ref[:, _lanes(c)] = z

    cos = cos_ref[...]
    sin = sin_ref[...]
    lane = lax.broadcasted_iota(jnp.int32, cos.shape, 1)
    half = head_dim // 2
    first_half = _mod_pow2(lane, head_dim) < half

    def rope(z):
        rot = jnp.where(first_half, pltpu.roll(z, LANES - half, 1), pltpu.roll(z, half, 1))
        return z * cos + rot * sin

    tiles = range(width // LANES)

    def diff_q(z):
        for c in tiles:
            qd_bf[:, _lanes(c)] = (rope(z[:, _lanes(c)]) * q_scale).astype(BF16)

    def diff_k(z):
        for c in tiles:
            rk = rope(z[:, _lanes(c)])
            put(kd_ref, c, rk)
            kd_bf[:, _lanes(c)] = rk.astype(BF16)

    def diff_v(z):
        vd_bf[...] = z.astype(BF16)
        if position_minor:
            for h in tiles:
                vd_ref[pl.ds(h, z.shape[0], stride=len(tiles)), :] = z[:, _lanes(h)]
        else:
            vd_ref[...] = z

    def stick_q(z):
        qs_bf[...] = (z * q_scale).astype(BF16)

    def stick_kv(out_ref, bf_ref):
        def post(z):
            bf_ref[...] = z.astype(BF16)
            for c in tiles:
                put(out_ref, c, z[:, _lanes(c)])
        return post

    def gate(out_ref, c):
        def post(z):
            out_ref[:, c * width:(c + 1) * width] = (1.0 / (1.0 + _exp(z, -1.0))).astype(BF16)
        return post

    posts = [diff_q, diff_k, diff_v, stick_q, stick_kv(ks_ref, ks_bf), stick_kv(vs_ref, vs_bf)]
    posts += [gate(ref, c) for ref in (ga_ref, gb_ref) for c in range(d_model // width)]
    z = {}

    def matmul(u):
        z[u] = proj(u * width, width)

    _staggered([matmul, lambda u: posts[u](z.pop(u))], len(posts))


def _in_proj(x2d, g, w_bf, cos_t, sin_t, *, head_dim, seq=None):
    n, d_model = x2d.shape
    n_in = w_bf.shape[1]
    width = d_model // 2
    tm = min(PROJ_ROWS, n)
    n_tab = cos_t.shape[0] // tm
    row = lambda i: (i, 0)
    tab = lambda i: (i % n_tab, 0)
    row_spec = pl.BlockSpec((tm, width), row)
    bf_out = jax.ShapeDtypeStruct((n, width), BF16)
    gate_out = jax.ShapeDtypeStruct((n, d_model), BF16)
    if seq is None:
        cache_specs = [row_spec] * 4
        cache_shapes = [jax.ShapeDtypeStruct((n, width), F32)] * 4
    else:
        nt = seq // tm
        minor_spec = pl.BlockSpec((None, width, tm), lambda i: (i // nt, 0, i % nt))
        minor_shape = jax.ShapeDtypeStruct((n // seq, width, seq), F32)
        heads = width // LANES
        cache_specs = [minor_spec, pl.BlockSpec((tm * heads, LANES), row), minor_spec, minor_spec]
        cache_shapes = [minor_shape, jax.ShapeDtypeStruct((n * heads, LANES), F32), minor_shape, minor_shape]
    body = functools.partial(_in_proj_body, width=width, d_model=d_model, head_dim=head_dim,
                             q_scale=head_dim ** -0.5, position_minor=seq is not None)
    return pl.pallas_call(
        body,
        grid=(n // tm,),
        in_specs=[pl.BlockSpec((tm, d_model), row),
                  _const_spec((1, d_model)),
                  _const_spec((d_model, n_in)),
                  pl.BlockSpec((tm, LANES), tab),
                  pl.BlockSpec((tm, LANES), tab)],
        out_specs=cache_specs + [row_spec] * 6 + [pl.BlockSpec((tm, d_model), row)] * 2,
        out_shape=cache_shapes + [bf_out] * 6 + [gate_out] * 2,
        compiler_params=_params(1),
        name="in_proj",
    )(x2d, g, w_bf, cos_t, sin_t)


def _rope_tables(pos, head_dim):
    half = head_dim // 2
    inv = ROPE_THETA ** (-jnp.arange(half, dtype=F32) * 2.0 / head_dim)
    ang = pos.astype(F32)[:, None] * inv[None, :]
    cos = jnp.cos(ang)
    sin = jnp.sin(ang)
    reps = LANES // head_dim
    return (jnp.tile(jnp.concatenate([cos, cos], -1), (1, reps)),
            jnp.tile(jnp.concatenate([-sin, sin], -1), (1, reps)))


def _stack_halves(q, half):
    lane = lax.broadcasted_iota(jnp.int32, q.shape, 1)
    zero = jnp.zeros_like(q)
    return jnp.concatenate([jnp.where(lane < half, q, zero), jnp.where(lane >= half, q, zero)], axis=0)


def _exp(x, sign=1.0):
    return jnp.exp2(x * (sign * LOG2_E))


def _softplus_parts(z):
    sp = jnp.maximum(z, 0.0) + jnp.log(1.0 + _exp(jnp.abs(z), -1.0))
    return sp, z - sp


def _split_bf16(x):
    hi = x.astype(BF16)
    return hi, (x - hi.astype(F32)).astype(BF16)


def _suffix_matrix(n, keys_on_rows):
    a = lax.broadcasted_iota(jnp.int32, (n, n), 0)
    b = lax.broadcasted_iota(jnp.int32, (n, n), 1)
    after = (b > a) if keys_on_rows else (a > b)
    return jnp.where(after, 1.0, 0.0).astype(BF16)


def _lambda_value(lq1, lk1, lq2, lk2):
    return (jnp.exp(jnp.sum(lq1[...] * lk1[...], axis=-1, keepdims=True))
            - jnp.exp(jnp.sum(lq2[...] * lk2[...], axis=-1, keepdims=True))
            + LAMBDA_INIT_LAYER0)


def _staggered(stages, n, order=None):
    order = order or [0] + list(range(len(stages) - 1, 0, -1))
    for t in range(n + len(stages) - 1):
        for s in order:
            if 0 <= t - s < n:
                stages[s](t - s)


def _softmax_weights_t(s, m, l, mask):
    if mask is not None:
        s = jnp.where(mask, s, NEG_INF)
    m_new = jnp.maximum(m, jnp.max(s, axis=0, keepdims=True))
    alpha = _exp(m - m_new)
    p = _exp(s - m_new)
    return m_new, alpha * l + jnp.sum(p, axis=0, keepdims=True), alpha, p.astype(BF16)


def _stick_parts_t(z, mask):
    sp, log_sig = _softplus_parts(z)
    if mask is not None:
        sp = jnp.where(mask, sp, 0.0)
    return log_sig, jnp.sum(sp, axis=0, keepdims=True), _split_bf16(sp)


def _stick_weights_t(log_sig, later, r, mask):
    a = _exp(log_sig - later - r)
    if mask is not None:
        a = jnp.where(mask, a, 0.0)
    return a.astype(BF16)


def _key_query_positions(n_keys, rows):
    s = lax.broadcasted_iota(jnp.int32, (n_keys, 2 * rows), 0)
    t = _mod_pow2(lax.broadcasted_iota(jnp.int32, (n_keys, 2 * rows), 1), rows)
    return s, t


def _diff_finish_t(acc, l, lam, g_col, rows):
    o = acc / l
    o = o[:, :rows] - lam * o[:, rows:]
    y = o * lax.rsqrt(jnp.mean(o * o, axis=0, keepdims=True) + NORM_EPS) * g_col
    return (y * (1.0 - LAMBDA_INIT_LAYER0)).T.astype(BF16)


def _stick_finish_t(acc, rows, half):
    row = lax.broadcasted_iota(jnp.int32, (LANES, rows), 0)
    return jnp.where(row < half, acc[:, :rows], acc[:, rows:]).T.astype(BF16)


def _transpose_values(v_ref, vt_ref):
    nblk, _, n = vt_ref.shape
    for c in range(nblk):
        vt_ref[c] = v_ref[c * n:(c + 1) * n, :].astype(F32).T.astype(BF16)


def _diff_prompt_body(lq1, lk1, lq2, lk2, g_ref, q_ref, k_ref, v_ref, o_ref, vt_ref, acc_ref, *, rows, half):
    i = pl.program_id(1)
    groups = q_ref.shape[1] // LANES

    @pl.when(i == 0)
    def _():
        _transpose_values(v_ref, vt_ref)

    qs = [_stack_halves(q_ref[:, _lanes(g)], half) for g in range(groups)]
    acc_ref[...] = jnp.zeros_like(acc_ref)

    def step(j, stats, mask):
        off = pl.multiple_of(j * rows, rows)
        scores, weights, out = {}, {}, {}

        def score(g):
            scores[g] = _dot_nt(k_ref[pl.ds(off, rows), _lanes(g)], qs[g])

        def weigh(g):
            m, l, alpha, p = _softmax_weights_t(scores.pop(g), *stats[g], mask)
            out[g] = (m, l)
            weights[g] = (alpha, p)

        def update(g):
            alpha, p = weights.pop(g)
            acc_ref[g] = alpha * acc_ref[g] + _dot(vt_ref[j, _lanes(g), :], p)

        _staggered([score, weigh, update], groups)
        return tuple(out[g] for g in range(groups))

    init = tuple((jnp.full((1, 2 * rows), NEG_INF, F32), jnp.zeros((1, 2 * rows), F32))
                 for _ in range(groups))
    stats = lax.fori_loop(0, i, lambda j, st: step(j, st, None), init)
    s, t = _key_query_positions(rows, rows)
    stats = step(i, stats, _div_pow2(s, CHUNK) <= _div_pow2(t, CHUNK))
    lam = _lambda_value(lq1, lk1, lq2, lk2)
    for g in range(groups):
        o_ref[:, _lanes(g)] = _diff_finish_t(acc_ref[g], stats[g][1], lam, g_ref[...], rows)


def _stick_prompt_body(q_ref, k_ref, v_ref, o_ref, vt_ref, acc_ref, *, rows, half):
    i = pl.program_id(1)
    groups = q_ref.shape[1] // LANES

    @pl.when(i == 0)
    def _():
        _transpose_values(v_ref, vt_ref)

    qs = [_stack_halves(q_ref[:, _lanes(g)], half) for g in range(groups)]
    suffix = _suffix_matrix(rows, keys_on_rows=True)
    acc_ref[...] = jnp.zeros_like(acc_ref)

    def step(j, carry, mask):
        off = pl.multiple_of(j * rows, rows)
        logits, parts, sums, weights, out = {}, {}, {}, {}, {}

        def logit(g):
            logits[g] = _dot_nt(k_ref[pl.ds(off, rows), _lanes(g)], qs[g])

        def split(g):
            log_sig, total, parts[g] = _stick_parts_t(logits.pop(g), mask)
            out[g] = carry[g] + total
            sums[g] = log_sig

        def later(g):
            hi, lo = parts.pop(g)
            sums[g] = (sums[g], _dot(suffix, hi) + _dot(suffix, lo))

        def weigh(g):
            weights[g] = _stick_weights_t(*sums.pop(g), carry[g], mask)

        def update(g):
            acc_ref[g] += _dot(vt_ref[j, _lanes(g), :], weights.pop(g))

        _staggered([logit, split, later, weigh, update], groups)
        return tuple(out[g] for g in range(groups))

    def live(carry):
        return jnp.min(functools.reduce(jnp.minimum, carry)) < DEAD_SOFTPLUS_SUM

    def older_block(state):
        jj, _, carry = state
        carry = step(i - 1 - jj, carry, None)
        return jj + 1, live(carry), carry

    s, t = _key_query_positions(rows, rows)
    carry = step(i, tuple(jnp.zeros((1, 2 * rows), F32) for _ in range(groups)), s < t)
    lax.while_loop(lambda state: (state[0] < i) & state[1], older_block, (jnp.int32(0), live(carry), carry))
    for g in range(groups):
        o_ref[:, _lanes(g)] = _stick_finish_t(acc_ref[g], rows, half)


def _prompt_attention(body, name, extra_inputs, extra_specs, q, k, v, *, batch, seq):
    n, width = q.shape
    rows = ATTN_ROWS
    nq = seq // rows
    q_spec = pl.BlockSpec((rows, width), lambda b, i: (b * nq + i, 0))
    kv_spec = pl.BlockSpec((seq, width), lambda b, i: (b, 0))
    return pl.pallas_call(
        body,
        grid=(batch, nq),
        in_specs=extra_specs + [q_spec, kv_spec, kv_spec],
        out_specs=q_spec,
        out_shape=jax.ShapeDtypeStruct((n, width), BF16),
        scratch_shapes=[pltpu.VMEM((nq, width, rows), BF16),
                        pltpu.VMEM((width // LANES, LANES, 2 * rows), F32)],
        compiler_params=_params(2),
        name=name,
    )(*extra_inputs, q, k, v)


def _pad_rows(x, n):
    return jnp.concatenate([x, jnp.zeros((n - x.shape[0], x.shape[1]), x.dtype)], axis=0)


def _stacked_positions(rows, n_keys):
    t = _mod_pow2(lax.broadcasted_iota(jnp.int32, (2 * rows, n_keys), 0), rows)
    s = lax.broadcasted_iota(jnp.int32, (2 * rows, n_keys), 1)
    return t, s


def _diff_decode_body(lq1, lk1, lq2, lk2, g_ref, q_ref, kn_ref, vn_ref, ckt_ref, cv_ref, o_ref,
                      *, rows, half):
    past = ckt_ref.shape[1]
    lam = _lambda_value(lq1, lk1, lq2, lk2)
    t, s = _stacked_positions(rows, LANES)
    new_mask = (s < rows) & (_div_pow2(past + s, CHUNK) <= _div_pow2(past + t, CHUNK))
    groups = q_ref.shape[1] // LANES
    scores, weights = {}, {}

    def score(g):
        qs = _stack_halves(q_ref[:, _lanes(g)], half)
        scores[g] = (_dot(qs, ckt_ref[_lanes(g), :].astype(BF16)),
                     _dot_nt(qs, _pad_rows(kn_ref[:, _lanes(g)], LANES)))

    def weigh(g):
        s_c, s_n = scores.pop(g)
        s_n = jnp.where(new_mask, s_n, NEG_INF)
        m = jnp.maximum(jnp.max(s_c, axis=1, keepdims=True), jnp.max(s_n, axis=1, keepdims=True))
        p_c = _exp(s_c - m)
        p_n = _exp(s_n - m)
        l = jnp.sum(p_c, axis=1, keepdims=True) + jnp.sum(p_n, axis=1, keepdims=True)
        weights[g] = (p_c.astype(BF16), p_n.astype(BF16), l)

    def update(g):
        p_c, p_n, l = weights.pop(g)
        v_c = cv_ref[pl.ds(g, past, stride=groups), :]
        o = (_dot(p_c, v_c.astype(BF16)) + _dot(p_n, _pad_rows(vn_ref[:, _lanes(g)], LANES))) / l
        o = o[:rows] - lam * o[rows:]
        o_ref[:, _lanes(g)] = (_rms(o, g_ref[...]) * (1.0 - LAMBDA_INIT_LAYER0)).astype(BF16)

    _staggered([score, weigh, update], groups)


def _stick_decode_body(q_ref, kn_ref, vn_ref, ckt_ref, cvt_ref, o_ref, *, rows, half):
    past = ckt_ref.shape[1]
    suffix = _suffix_matrix(DECODE_KEYS, keys_on_rows=False)
    suffix_new = _suffix_matrix(LANES, keys_on_rows=False)
    t, s = _stacked_positions(rows, LANES)
    new_mask = (s < rows) & (s < t)
    lane = lax.broadcasted_iota(jnp.int32, (rows, LANES), 1)
    blocks = [slice(b * DECODE_KEYS, (b + 1) * DECODE_KEYS) for b in range(past // DECODE_KEYS)]
    logits, parts, laters, weights = {}, {}, {}, {}

    def logit(g):
        qs = _stack_halves(q_ref[:, _lanes(g)], half)
        logits[g] = (_dot_nt(qs, _pad_rows(kn_ref[:, _lanes(g)], LANES)),
                     _dot(qs, ckt_ref[_lanes(g), :].astype(BF16)))

    def split(g):
        z_n, z_c = logits.pop(g)
        sp_n, log_sig_n = _softplus_parts(z_n)
        sp_n = jnp.where(new_mask, sp_n, 0.0)
        sp_c, log_sig_c = _softplus_parts(z_c)
        carries = [jnp.sum(sp_n, axis=1, keepdims=True)]
        for cols in reversed(blocks[1:]):
            carries.append(carries[-1] + jnp.sum(sp_c[:, cols], axis=1, keepdims=True))
        parts[g] = (_split_bf16(sp_n), _split_bf16(sp_c))
        laters[g] = (log_sig_n, log_sig_c, carries[::-1])

    def later(g):
        (hi_n, lo_n), (hi_c, lo_c) = parts.pop(g)
        laters[g] += (_dot(hi_n, suffix_new) + _dot(lo_n, suffix_new),
                      [_dot(hi_c[:, cols], suffix) + _dot(lo_c[:, cols], suffix) for cols in blocks])

    def weigh(g):
        log_sig_n, log_sig_c, carries, later_n, later_c = laters.pop(g)
        a_n = jnp.where(new_mask, _exp(log_sig_n - later_n), 0.0).astype(BF16)
        weights[g] = (a_n, [_exp(log_sig_c[:, cols] - later_c[b] - carries[b]).astype(BF16)
                            for b, cols in enumerate(blocks)])

    def update(g):
        a_n, a_c = weights.pop(g)
        vt = cvt_ref[_lanes(g), :].astype(BF16)
        acc = _dot(a_n, _pad_rows(vn_ref[:, _lanes(g)], LANES))
        for b, cols in enumerate(blocks):
            acc = acc + _dot_nt(a_c[b], vt[:, cols])
        o_ref[:, _lanes(g)] = jnp.where(lane < half, acc[:rows], acc[rows:]).astype(BF16)

    _staggered([logit, split, later, weigh, update], q_ref.shape[1] // LANES)


def _decode_attention(body, name, extra_inputs, extra_specs, q, k_new, v_new, cache_kt, cache_v, *, rows):
    n, width = q.shape
    new_spec = pl.BlockSpec((rows, width), lambda b: (b, 0))
    cache_spec = lambda c: pl.BlockSpec((None,) + c.shape[1:], lambda b: (b,) + (0,) * (c.ndim - 1))
    return pl.pallas_call(
        body,
        grid=(n // rows,),
        in_specs=extra_specs + [new_spec, new_spec, new_spec, cache_spec(cache_kt), cache_spec(cache_v)],
        out_specs=new_spec,
        out_shape=jax.ShapeDtypeStruct((n, width), BF16),
        compiler_params=_params(1),
        name=name,
    )(*extra_inputs, q, k_new, v_new, cache_kt, cache_v)


def _merge_ffn_body(*refs, d_ff, seq_rows):
    (x_ref, oa_ref, ob_ref, ga_ref, gb_ref, wa_ref, wb_ref, wo_ref, fg_ref, wup_ref, cw_ref, cb_ref,
     wdn_ref, ng_ref) = refs[:14]
    if seq_rows is None:
        y_ref, conv_ref, acc_ref, carry_ref, shift_ref = refs[14:]
    else:
        h0_ref, h1_ref, y_ref, last2_ref, last1_ref, acc_ref, u_ref = refs[14:]
    tm = x_ref.shape[0]
    assert cw_ref.shape[0] == 3

    mix = (ga_ref[...].astype(F32) * _dot(oa_ref[...], wa_ref[...])
           + gb_ref[...].astype(F32) * _dot(ob_ref[...], wb_ref[...]))
    x1 = x_ref[...] + _dot(mix.astype(BF16), wo_ref[...])
    xn = _rms(x1, fg_ref[...]).astype(BF16)

    if seq_rows is None:
        @pl.when(pl.program_id(1) == 0)
        def _():
            carry_ref[...] = jnp.zeros_like(carry_ref)
    else:
        pos = _mod_pow2(lax.broadcasted_iota(jnp.int32, (tm, FFN_COLS), 0), seq_rows)

    def taps(u, u1, u2, cols):
        return cb_ref[:, cols] + cw_ref[0:1, cols] * u2 + cw_ref[1:2, cols] * u1 + cw_ref[2:3, cols] * u

    def conv(u, cols):
        if seq_rows is None:
            stage = shift_ref.at[cols.start // d_ff]
            stage[0:8, :] = carry_ref[:, cols]
            stage[8:8 + tm, :] = u
            carry_ref[:, cols] = u[tm - 8:, :]
            conv_ref[:, cols] = u[tm - 8:, :]
            return taps(u, stage[7:7 + tm, :], stage[6:6 + tm, :], cols)
        else:
            n_seq = tm // seq_rows
            expand = lambda h: jnp.broadcast_to(h[:, None, :], (n_seq, seq_rows, FFN_COLS)).reshape(tm, FFN_COLS)
            prev2 = expand(h0_ref[:, cols])
            prev1 = expand(h1_ref[:, cols])
            for part in range(FFN_COLS // LANES):
                lanes = _lanes(part)
                out = slice(cols.start + part * LANES, cols.start + (part + 1) * LANES)
                u_ref[...] = u[:, lanes]
                last2_ref[:, out] = u_ref[pl.ds(seq_rows - 2, n_seq, stride=seq_rows), :]
                last1_ref[:, out] = u_ref[pl.ds(seq_rows - 1, n_seq, stride=seq_rows), :]
            u1 = jnp.where(pos == 0, prev1, pltpu.roll(u, 1, 0))
            u2 = jnp.where(pos == 0, prev2, jnp.where(pos == 1, prev1, pltpu.roll(u, 2, 0)))
            return taps(u, u1, u2, cols)

    gate_cols = lambda c: slice(c * FFN_COLS, (c + 1) * FFN_COLS)
    val_cols = lambda c: slice(d_ff + c * FFN_COLS, d_ff + (c + 1) * FFN_COLS)
    ups, hidden = {}, {}

    def up(c):
        ups[c] = _dot(xn, wup_ref[:, gate_cols(c)]), _dot(xn, wup_ref[:, val_cols(c)])

    def gated(c):
        ug, uv = ups.pop(c)
        gate = conv(ug, gate_cols(c))
        hidden[c] = (gate * conv(uv, val_cols(c)) / (1.0 + _exp(gate, -1.0))).astype(BF16)

    def down(c):
        part = _dot(hidden.pop(c), wdn_ref[gate_cols(c), :])
        if c == 0:
            acc_ref[...] = part
        else:
            acc_ref[...] += part

    _staggered([up, gated, down], d_ff // FFN_COLS)
    y_ref[...] = _rms(x1 + acc_ref[...], ng_ref[...])


def _merge_ffn(x2d, oa, ob, ga, gb, wa, wb, wo, fg, wup, cw, cb, wdn, ng, *, batch, history=None):
    n, d_model = x2d.shape
    width = oa.shape[1]
    d_ff = wdn.shape[0]
    two_ff = wup.shape[1]
    seq = n // batch
    consts = [wa, wb, wo, fg, wup, cw, cb, wdn, ng]
    const_specs = [_const_spec(a.shape) for a in consts]
    if history is None:
        tm = min(FFN_ROWS, seq)
        nt = seq // tm
        grid = (batch, nt)
        row = lambda b, j: (b * nt + j, 0)
        extra, extra_specs = [], []
        conv_shape = [jax.ShapeDtypeStruct((batch, 8, two_ff), F32)]
        conv_spec = [pl.BlockSpec((None, 8, two_ff), lambda b, j: (b, 0, 0))]
        scratch = [pltpu.VMEM((tm, d_model), F32), pltpu.VMEM((8, two_ff), F32),
                   pltpu.VMEM((2, tm + 8, FFN_COLS), F32)]
        seq_rows = None
    else:
        tm = n
        grid = (1,)
        row = lambda i: (0, 0)
        extra = list(history)
        extra_specs = [_const_spec(h.shape) for h in history]
        conv_shape = [jax.ShapeDtypeStruct((batch, two_ff), F32)] * 2
        conv_spec = [pl.BlockSpec((batch, two_ff), row)] * 2
        scratch = [pltpu.VMEM((tm, d_model), F32), pltpu.VMEM((tm, LANES), F32)]
        seq_rows = seq
    tile = lambda cols: pl.BlockSpec((tm, cols), row)
    return pl.pallas_call(
        functools.partial(_merge_ffn_body, d_ff=d_ff, seq_rows=seq_rows),
        grid=grid,
        in_specs=[tile(d_model), tile(width), tile(width), tile(d_model), tile(d_model)]
                 + const_specs + extra_specs,
        out_specs=[tile(d_model)] + conv_spec,
        out_shape=[jax.ShapeDtypeStruct((n, d_model), F32)] + conv_shape,
        scratch_shapes=scratch,
        compiler_params=_params(len(grid)),
        name="merge_ffn" if history is None else "merge_ffn_streams",
    )(x2d, oa, ob, ga, gb, *consts, *extra)


def kernel(x_prompt, x_sample, cache_diff_k, cache_diff_v, cache_sb_k, cache_sb_v, state_conv,
           attn_norm_g, w_in, lambda_q1, lambda_k1, lambda_q2, lambda_k2, subln_g, w_branch_a,
           w_branch_b, w_out, ffn_norm_g, w_up, conv_w, conv_b, w_down, final_norm_g):
    depth = w_in.shape[0]
    assert depth == 1, "single-layer trunk only"
    batch, seq, d_model = x_prompt.shape
    dec_batch, dec_seq, _ = x_sample.shape
    past = cache_diff_k.shape[2]
    width = d_model // 2
    da_heads, _, da_dim = cache_diff_k.shape[3:]
    da_v = cache_diff_v.shape[-1]
    sb_heads, sb_dim = cache_sb_k.shape[3:]
    assert da_dim == sb_dim and 2 * da_dim == LANES and da_v == LANES
    half = da_dim

    bf = lambda w: w[0].astype(BF16)
    row1 = lambda v: v.reshape(1, -1)
    w_in_bf = bf(w_in)
    consts = (bf(w_branch_a), bf(w_branch_b), bf(w_out), ffn_norm_g, bf(w_up), conv_w[0], conv_b,
              bf(w_down), row1(final_norm_g))
    lams = [row1(v[0]) for v in (lambda_q1, lambda_k1, lambda_q2, lambda_k2)]
    g_row, g_col = row1(subln_g[0]), subln_g[0].reshape(-1, 1)
    specs = lambda arrays: [_const_spec(a.shape) for a in arrays]

    xp = x_prompt.reshape(batch * seq, d_model)
    cos_p, sin_p = _rope_tables(jnp.arange(seq), da_dim)
    (p_dk, p_dv, p_sk, p_sv, qd, kd, vd, qs, ks, vs, ga, gb) = _in_proj(
        xp, attn_norm_g, w_in_bf, cos_p, sin_p, head_dim=da_dim, seq=seq)
    major = lambda c, *dims: jnp.moveaxis(c.reshape(batch, *dims, seq), -1, 1)
    p_dk, p_sk, p_sv = (major(p_dk, da_heads, 2, da_dim), major(p_sk, sb_heads, sb_dim),
                        major(p_sv, sb_heads, sb_dim))
    oa = _prompt_attention(functools.partial(_diff_prompt_body, rows=ATTN_ROWS, half=half), "diff_prompt",
                           lams + [g_col], specs(lams + [g_col]), qd, kd, vd, batch=batch, seq=seq)
    ob = _prompt_attention(functools.partial(_stick_prompt_body, rows=ATTN_ROWS, half=half), "stick_prompt",
                           [], [], qs, ks, vs, batch=batch, seq=seq)
    y_p, conv_p = _merge_ffn(xp, oa, ob, ga, gb, *consts, batch=batch)

    xs = x_sample.reshape(dec_batch * dec_seq, d_model)
    cos_s, sin_s = _rope_tables(past + jnp.arange(dec_seq), da_dim)
    cos_s, sin_s = jnp.tile(cos_s, (dec_batch, 1)), jnp.tile(sin_s, (dec_batch, 1))
    (s_dk, s_dv, s_sk, s_sv, qd, kd, vd, qs, ks, vs, ga, gb) = _in_proj(
        xs, attn_norm_g, w_in_bf, cos_s, sin_s, head_dim=da_dim)
    minor = lambda c: jnp.moveaxis(c[0], 1, -1).reshape(dec_batch, width, past)
    oa = _decode_attention(functools.partial(_diff_decode_body, rows=dec_seq, half=half), "diff_decode",
                           lams + [g_row], specs(lams + [g_row]), qd, kd, vd,
                           minor(cache_diff_k), cache_diff_v[0].reshape(dec_batch, past * da_heads, da_v),
                           rows=dec_seq)
    ob = _decode_attention(functools.partial(_stick_decode_body, rows=dec_seq, half=half), "stick_decode",
                           [], [], qs, ks, vs, minor(cache_sb_k), minor(cache_sb_v), rows=dec_seq)
    y_s, last2, last1 = _merge_ffn(xs, oa, ob, ga, gb, *consts, batch=dec_batch,
                                   history=(state_conv[0, :, 0], state_conv[0, :, 1]))

    conv_p = conv_p[:, 6:]
    conv_s = jnp.stack([last2, last1], axis=1)
    return (y_p.reshape(batch, seq, d_model), y_s.reshape(dec_batch, dec_seq, d_model),
            p_dk[None], p_dv.reshape(depth, batch, seq, da_heads, da_v), p_sk[None], p_sv[None],
            conv_p[None],
            s_dk.reshape(depth, dec_batch, dec_seq, da_heads, 2, da_dim),
            s_dv.reshape(depth, dec_batch, dec_seq, da_heads, da_v),
            s_sk.reshape(depth, dec_batch, dec_seq, sb_heads, sb_dim),
            s_sv.reshape(depth, dec_batch, dec_seq, sb_heads, sb_dim),
            conv_s[None])
```

```python
import functools

import jax
import jax.numpy as jnp
from jax import lax
from jax.experimental import pallas as pl
from jax.experimental.pallas import tpu as pltpu

F32 = jnp.float32
BF16 = jnp.bfloat16

CHUNK = 64
ROPE_THETA = 10000.0
NORM_EPS = 1e-6
NEG_INF = -1e30
LAMBDA_INIT_LAYER0 = 0.8 - 0.6 * 1.0
LOG2_E = 1.4426950408889634
DEAD_SOFTPLUS_SUM = 104.0

LANES = 128
VMEM_LIMIT_BYTES = 56 * 1024 * 1024

PROJ_ROWS = 512
DIFF_ROWS = 512
STICK_ROWS = 256
FFN_ROWS = 512
FFN_COLS = 256
DECODE_KEYS = 256


def _rms(x, g):
    return x * lax.rsqrt(jnp.mean(x * x, axis=-1, keepdims=True) + NORM_EPS) * g


def _dot(a, b):
    return jnp.dot(a, b, preferred_element_type=F32)


def _dot_nt(a, b):
    return lax.dot_general(a, b, (((1,), (1,)), ((), ())), preferred_element_type=F32)


def _log2(n):
    assert n > 0 and n & (n - 1) == 0, n
    return n.bit_length() - 1


def _mod_pow2(x, n):
    _log2(n)
    return jnp.bitwise_and(x, n - 1)


def _div_pow2(x, n):
    return jnp.right_shift(x, _log2(n))


def _lanes(g):
    return slice(g * LANES, (g + 1) * LANES)


def _const_spec(shape):
    return pl.BlockSpec(shape, lambda *_: (0,) * len(shape), pipeline_mode=pl.Buffered(1))


def _params(n_axes, **flags):
    return pltpu.CompilerParams(dimension_semantics=("arbitrary",) * n_axes,
                                vmem_limit_bytes=VMEM_LIMIT_BYTES, flags=flags or None)


def _in_proj_body(x_ref, g_ref, w_ref, cos_ref, sin_ref,
                  kd_ref, vd_ref, ks_ref, vs_ref,
                  qd_bf, kd_bf, vd_bf, qs_bf, ks_bf, vs_bf, ga_ref, gb_ref,
                  *, width, d_model, head_dim, q_scale, position_minor):
    xb = _rms(x_ref[...], g_ref[...]).astype(BF16)

    def proj(c0, n):
        return _dot(xb, w_ref[:, c0:c0 + n])

    def put(ref, c, z):
        if position_minor:
            ref[_lanes(c), :] = z.T
        else:
            ref[:, _lanes(c)] = z

    cos = cos_ref[...]
    sin = sin_ref[...]
    lane = lax.broadcasted_iota(jnp.int32, cos.shape, 1)
    half = head_dim // 2
    first_half = _mod_pow2(lane, head_dim) < half

    def rope(z):
        rot = jnp.where(first_half, pltpu.roll(z, LANES - half, 1), pltpu.roll(z, half, 1))
        return z * cos + rot * sin

    tiles = range(width // LANES)

    def diff_q(z):
        for c in tiles:
            qd_bf[:, _lanes(c)] = (rope(z[:, _lanes(c)]) * q_scale).astype(BF16)

    def diff_k(z):
        for c in tiles:
            rk = rope(z[:, _lanes(c)])
            put(kd_ref, c, rk)
            kd_bf[:, _lanes(c)] = rk.astype(BF16)

    def diff_v(z):
        vd_bf[...] = z.astype(BF16)
        if position_minor:
            for h in tiles:
                vd_ref[pl.ds(h, z.shape[0], stride=len(tiles)), :] = z[:, _lanes(h)]
        else:
            vd_ref[...] = z

    def stick_q(z):
        qs_bf[...] = (z * q_scale).astype(BF16)

    def stick_kv(out_ref, bf_ref):
        def post(z):
            bf_ref[...] = z.astype(BF16)
            for c in tiles:
                put(out_ref, c, z[:, _lanes(c)])
        return post

    def gate(out_ref, c):
        def post(z):
            out_ref[:, c * width:(c + 1) * width] = (1.0 / (1.0 + _exp(z, -1.0))).astype(BF16)
        return post

    posts = [diff_q, diff_k, diff_v, stick_q, stick_kv(ks_ref, ks_bf), stick_kv(vs_ref, vs_bf)]
    posts += [gate(ref, c) for ref in (ga_ref, gb_ref) for c in range(d_model // width)]
    z = {}

    def matmul(u):
        z[u] = proj(u * width, width)

    _staggered([matmul, lambda u: posts[u](z.pop(u))], len(posts))


def _in_proj(x2d, g, w_bf, cos_t, sin_t, *, head_dim, seq=None):
    n, d_model = x2d.shape
    n_in = w_bf.shape[1]
    width = d_model // 2
    tm = min(PROJ_ROWS, n)
    n_tab = cos_t.shape[0] // tm
    row = lambda i: (i, 0)
    tab = lambda i: (i % n_tab, 0)
    row_spec = pl.BlockSpec((tm, width), row)
    bf_out = jax.ShapeDtypeStruct((n, width), BF16)
    gate_out = jax.ShapeDtypeStruct((n, d_model), BF16)
    if seq is None:
        cache_specs = [row_spec] * 4
        cache_shapes = [jax.ShapeDtypeStruct((n, width), F32)] * 4
    else:
        nt = seq // tm
        minor_spec = pl.BlockSpec((None, width, tm), lambda i: (i // nt, 0, i % nt))
        minor_shape = jax.ShapeDtypeStruct((n // seq, width, seq), F32)
        heads = width // LANES
        cache_specs = [minor_spec, pl.BlockSpec((tm * heads, LANES), row), minor_spec, minor_spec]
        cache_shapes = [minor_shape, jax.ShapeDtypeStruct((n * heads, LANES), F32), minor_shape, minor_shape]
    body = functools.partial(_in_proj_body, width=width, d_model=d_model, head_dim=head_dim,
                             q_scale=head_dim ** -0.5, position_minor=seq is not None)
    return pl.pallas_call(
        body,
        grid=(n // tm,),
        in_specs=[pl.BlockSpec((tm, d_model), row),
                  _const_spec((1, d_model)),
                  _const_spec((d_model, n_in)),
                  pl.BlockSpec((tm, LANES), tab),
                  pl.BlockSpec((tm, LANES), tab)],
        out_specs=cache_specs + [row_spec] * 6 + [pl.BlockSpec((tm, d_model), row)] * 2,
        out_shape=cache_shapes + [bf_out] * 6 + [gate_out] * 2,
        compiler_params=_params(1),
        name="in_proj",
    )(x2d, g, w_bf, cos_t, sin_t)


def _rope_tables(pos, head_dim):
    half = head_dim // 2
    inv = ROPE_THETA ** (-jnp.arange(half, dtype=F32) * 2.0 / head_dim)
    ang = pos.astype(F32)[:, None] * inv[None, :]
    cos = jnp.cos(ang)
    sin = jnp.sin(ang)
    reps = LANES // head_dim
    return (jnp.tile(jnp.concatenate([cos, cos], -1), (1, reps)),
            jnp.tile(jnp.concatenate([-sin, sin], -1), (1, reps)))


def _stack_halves(q, half):
    lane = lax.broadcasted_iota(jnp.int32, q.shape, 1)
    zero = jnp.zeros_like(q)
    return jnp.concatenate([jnp.where(lane < half, q, zero), jnp.where(lane >= half, q, zero)], axis=0)


def _exp(x, sign=1.0):
    return jnp.exp2(x * (sign * LOG2_E))


def _softplus_parts(z):
    sp = jnp.maximum(z, 0.0) + jnp.log(1.0 + _exp(jnp.abs(z), -1.0))
    return sp, z - sp


def _split_bf16(x):
    hi = x.astype(BF16)
    return hi, (x - hi.astype(F32)).astype(BF16)


def _suffix_matrix(n, keys_on_rows):
    a = lax.broadcasted_iota(jnp.int32, (n, n), 0)
    b = lax.broadcasted_iota(jnp.int32, (n, n), 1)
    after = (b > a) if keys_on_rows else (a > b)
    return jnp.where(after, 1.0, 0.0).astype(BF16)


def _lambda_value(lq1, lk1, lq2, lk2):
    return (jnp.exp(jnp.sum(lq1[...] * lk1[...], axis=-1, keepdims=True))
            - jnp.exp(jnp.sum(lq2[...] * lk2[...], axis=-1, keepdims=True))
            + LAMBDA_INIT_LAYER0)


def _staggered(stages, n, order=None):
    order = order or [0] + list(range(len(stages) - 1, 0, -1))
    for t in range(n + len(stages) - 1):
        for s in order:
            if 0 <= t - s < n:
                stages[s](t - s)


def _softmax_weights_t(s, m, l, mask):
    if mask is not None:
        s = jnp.where(mask, s, NEG_INF)
    m_new = jnp.maximum(m, jnp.max(s, axis=0, keepdims=True))
    alpha = _exp(m - m_new)
    p = _exp(s - m_new)
    return m_new, alpha * l + jnp.sum(p, axis=0, keepdims=True), alpha, p.astype(BF16)


def _stick_parts_t(z, mask):
    sp, log_sig = _softplus_parts(z)
    if mask is not None:
        sp = jnp.where(mask, sp, 0.0)
    return log_sig, jnp.sum(sp, axis=0, keepdims=True), _split_bf16(sp)


def _stick_weights_t(log_sig, later, r, mask):
    a = _exp(log_sig - later - r)
    if mask is not None:
        a = jnp.where(mask, a, 0.0)
    return a.astype(BF16)


def _key_query_positions(n_keys, rows):
    s = lax.broadcasted_iota(jnp.int32, (n_keys, 2 * rows), 0)
    t = _mod_pow2(lax.broadcasted_iota(jnp.int32, (n_keys, 2 * rows), 1), rows)
    return s, t


def _diff_finish_t(acc, l, lam, g_col, rows):
    o = acc / l
    o = o[:, :rows] - lam * o[:, rows:]
    y = o * lax.rsqrt(jnp.mean(o * o, axis=0, keepdims=True) + NORM_EPS) * g_col
    return (y * (1.0 - LAMBDA_INIT_LAYER0)).T.astype(BF16)


def _stick_finish_t(acc, rows, half):
    row = lax.broadcasted_iota(jnp.int32, (LANES, rows), 0)
    return jnp.where(row < half, acc[:, :rows], acc[:, rows:]).T.astype(BF16)


def _transpose_values(v_ref, vt_ref):
    nblk, _, n = vt_ref.shape
    for c in range(nblk):
        vt_ref[c] = v_ref[c * n:(c + 1) * n, :].astype(F32).T.astype(BF16)


def _diff_prompt_body(lq1, lk1, lq2, lk2, g_ref, q_ref, k_ref, v_ref, o_ref, vt_ref, acc_ref, *, rows, half):
    i = pl.program_id(1)
    groups = q_ref.shape[1] // LANES

    @pl.when(i == 0)
    def _():
        _transpose_values(v_ref, vt_ref)

    qs = [_stack_halves(q_ref[:, _lanes(g)], half) for g in range(groups)]
    acc_ref[...] = jnp.zeros_like(acc_ref)

    def step(j, stats, mask):
        off = pl.multiple_of(j * rows, rows)
        scores, weights, out = {}, {}, {}

        def score(g):
            scores[g] = _dot_nt(k_ref[pl.ds(off, rows), _lanes(g)], qs[g])

        def weigh(g):
            m, l, alpha, p = _softmax_weights_t(scores.pop(g), *stats[g], mask)
            out[g] = (m, l)
            weights[g] = (alpha, p)

        def update(g):
            alpha, p = weights.pop(g)
            acc_ref[g] = alpha * acc_ref[g] + _dot(vt_ref[j, _lanes(g), :], p)

        _staggered([score, weigh, update], groups)
        return tuple(out[g] for g in range(groups))

    init = tuple((jnp.full((1, 2 * rows), NEG_INF, F32), jnp.zeros((1, 2 * rows), F32))
                 for _ in range(groups))
    stats = lax.fori_loop(0, i, lambda j, st: step(j, st, None), init)
    s, t = _key_query_positions(rows, rows)
    stats = step(i, stats, _div_pow2(s, CHUNK) <= _div_pow2(t, CHUNK))
    lam = _lambda_value(lq1, lk1, lq2, lk2)
    for g in range(groups):
        o_ref[:, _lanes(g)] = _diff_finish_t(acc_ref[g], stats[g][1], lam, g_ref[...], rows)


def _stick_prompt_body(q_ref, k_ref, v_ref, o_ref, vt_ref, acc_ref, *, rows, half):
    i = pl.program_id(1)
    groups = q_ref.shape[1] // LANES

    @pl.when(i == 0)
    def _():
        _transpose_values(v_ref, vt_ref)

    qs = [_stack_halves(q_ref[:, _lanes(g)], half) for g in range(groups)]
    suffix = _suffix_matrix(rows, keys_on_rows=True)
    acc_ref[...] = jnp.zeros_like(acc_ref)

    def step(j, carry, mask):
        off = pl.multiple_of(j * rows, rows)
        logits, parts, sums, weights, out = {}, {}, {}, {}, {}

        def logit(g):
            logits[g] = _dot_nt(k_ref[pl.ds(off, rows), _lanes(g)], qs[g])

        def split(g):
            log_sig, total, parts[g] = _stick_parts_t(logits.pop(g), mask)
            out[g] = carry[g] + total
            sums[g] = log_sig

        def later(g):
            hi, lo = parts.pop(g)
            sums[g] = (sums[g], _dot(suffix, hi) + _dot(suffix, lo))

        def weigh(g):
            weights[g] = _stick_weights_t(*sums.pop(g), carry[g], mask)

        def update(g):
            acc_ref[g] += _dot(vt_ref[j, _lanes(g), :], weights.pop(g))

        _staggered([logit, split, later, weigh, update], groups)
        return tuple(out[g] for g in range(groups))

    def live(carry):
        return jnp.min(functools.reduce(jnp.minimum, carry)) < DEAD_SOFTPLUS_SUM

    def older_block(state):
        jj, _, carry = state
        carry = step(i - 1 - jj, carry, None)
        return jj + 1, live(carry), carry

    s, t = _key_query_positions(rows, rows)
    carry = step(i, tuple(jnp.zeros((1, 2 * rows), F32) for _ in range(groups)), s < t)
    lax.while_loop(lambda state: (state[0] < i) & state[1], older_block, (jnp.int32(0), live(carry), carry))
    for g in range(groups):
        o_ref[:, _lanes(g)] = _stick_finish_t(acc_ref[g], rows, half)


def _prompt_attention(body, name, extra_inputs, extra_specs, q, k, v, *, batch, seq, rows, half):
    n, width = q.shape
    body = functools.partial(body, rows=rows, half=half)
    nq = seq // rows
    q_spec = pl.BlockSpec((rows, width), lambda b, i: (b * nq + i, 0))
    kv_spec = pl.BlockSpec((seq, width), lambda b, i: (b, 0))
    return pl.pallas_call(
        body,
        grid=(batch, nq),
        in_specs=extra_specs + [q_spec, kv_spec, kv_spec],
        out_specs=q_spec,
        out_shape=jax.ShapeDtypeStruct((n, width), BF16),
        scratch_shapes=[pltpu.VMEM((nq, width, rows), BF16),
                        pltpu.VMEM((width // LANES, LANES, 2 * rows), F32)],
        compiler_params=_params(2),
        name=name,
    )(*extra_inputs, q, k, v)


def _pad_rows(x, n):
    return jnp.concatenate([x, jnp.zeros((n - x.shape[0], x.shape[1]), x.dtype)], axis=0)


def _stacked_positions(rows, n_keys):
    t = _mod_pow2(lax.broadcasted_iota(jnp.int32, (2 * rows, n_keys), 0), rows)
    s = lax.broadcasted_iota(jnp.int32, (2 * rows, n_keys), 1)
    return t, s


def _diff_decode_body(lq1, lk1, lq2, lk2, g_ref, q_ref, kn_ref, vn_ref, ckt_ref, cv_ref, o_ref,
                      *, rows, half):
    past = ckt_ref.shape[1]
    lam = _lambda_value(lq1, lk1, lq2, lk2)
    t, s = _stacked_positions(rows, LANES)
    new_mask = (s < rows) & (_div_pow2(past + s, CHUNK) <= _div_pow2(past + t, CHUNK))
    groups = q_ref.shape[1] // LANES
    scores, weights = {}, {}

    def score(g):
        qs = _stack_halves(q_ref[:, _lanes(g)], half)
        scores[g] = (_dot(qs, ckt_ref[_lanes(g), :].astype(BF16)),
                     _dot_nt(qs, _pad_rows(kn_ref[:, _lanes(g)], LANES)))

    def weigh(g):
        s_c, s_n = scores.pop(g)
        s_n = jnp.where(new_mask, s_n, NEG_INF)
        m = jnp.maximum(jnp.max(s_c, axis=1, keepdims=True), jnp.max(s_n, axis=1, keepdims=True))
        p_c = _exp(s_c - m)
        p_n = _exp(s_n - m)
        l = jnp.sum(p_c, axis=1, keepdims=True) + jnp.sum(p_n, axis=1, keepdims=True)
        weights[g] = (p_c.astype(BF16), p_n.astype(BF16), l)

    def update(g):
        p_c, p_n, l = weights.pop(g)
        v_c = cv_ref[pl.ds(g, past, stride=groups), :]
        o = (_dot(p_c, v_c.astype(BF16)) + _dot(p_n, _pad_rows(vn_ref[:, _lanes(g)], LANES))) / l
        o = o[:rows] - lam * o[rows:]
        o_ref[:, _lanes(g)] = (_rms(o, g_ref[...]) * (1.0 - LAMBDA_INIT_LAYER0)).astype(BF16)

    _staggered([score, weigh, update], groups)


def _stick_decode_body(q_ref, kn_ref, vn_ref, ckt_ref, cvt_ref, o_ref, *, rows, half):
    past = ckt_ref.shape[1]
    suffix = _suffix_matrix(DECODE_KEYS, keys_on_rows=False)
    suffix_new = _suffix_matrix(LANES, keys_on_rows=False)
    t, s = _stacked_positions(rows, LANES)
    new_mask = (s < rows) & (s < t)
    lane = lax.broadcasted_iota(jnp.int32, (rows, LANES), 1)
    blocks = [slice(b * DECODE_KEYS, (b + 1) * DECODE_KEYS) for b in range(past // DECODE_KEYS)]
    logits, parts, laters, weights = {}, {}, {}, {}

    def logit(g):
        qs = _stack_halves(q_ref[:, _lanes(g)], half)
        logits[g] = (_dot_nt(qs, _pad_rows(kn_ref[:, _lanes(g)], LANES)),
                     _dot(qs, ckt_ref[_lanes(g), :].astype(BF16)))

    def split(g):
        z_n, z_c = logits.pop(g)
        sp_n, log_sig_n = _softplus_parts(z_n)
        sp_n = jnp.where(new_mask, sp_n, 0.0)
        sp_c, log_sig_c = _softplus_parts(z_c)
        carries = [jnp.sum(sp_n, axis=1, keepdims=True)]
        for cols in reversed(blocks[1:]):
            carries.append(carries[-1] + jnp.sum(sp_c[:, cols], axis=1, keepdims=True))
        parts[g] = (_split_bf16(sp_n), _split_bf16(sp_c))
        laters[g] = (log_sig_n, log_sig_c, carries[::-1])

    def later(g):
        (hi_n, lo_n), (hi_c, lo_c) = parts.pop(g)
        laters[g] += (_dot(hi_n, suffix_new) + _dot(lo_n, suffix_new),
                      [_dot(hi_c[:, cols], suffix) + _dot(lo_c[:, cols], suffix) for cols in blocks])

    def weigh(g):
        log_sig_n, log_sig_c, carries, later_n, later_c = laters.pop(g)
        a_n = jnp.where(new_mask, _exp(log_sig_n - later_n), 0.0).astype(BF16)
        weights[g] = (a_n, [_exp(log_sig_c[:, cols] - later_c[b] - carries[b]).astype(BF16)
                            for b, cols in enumerate(blocks)])

    def update(g):
        a_n, a_c = weights.pop(g)
        vt = cvt_ref[_lanes(g), :].astype(BF16)
        acc = _dot(a_n, _pad_rows(vn_ref[:, _lanes(g)], LANES))
        for b, cols in enumerate(blocks):
            acc = acc + _dot_nt(a_c[b], vt[:, cols])
        o_ref[:, _lanes(g)] = jnp.where(lane < half, acc[:rows], acc[rows:]).astype(BF16)

    _staggered([logit, split, later, weigh, update], q_ref.shape[1] // LANES)


def _decode_attention(body, name, extra_inputs, extra_specs, q, k_new, v_new, cache_kt, cache_v, *, rows):
    n, width = q.shape
    new_spec = pl.BlockSpec((rows, width), lambda b: (b, 0))
    cache_spec = lambda c: pl.BlockSpec((None,) + c.shape[1:], lambda b: (b,) + (0,) * (c.ndim - 1))
    return pl.pallas_call(
        body,
        grid=(n // rows,),
        in_specs=extra_specs + [new_spec, new_spec, new_spec, cache_spec(cache_kt), cache_spec(cache_v)],
        out_specs=new_spec,
        out_shape=jax.ShapeDtypeStruct((n, width), BF16),
        compiler_params=_params(1),
        name=name,
    )(*extra_inputs, q, k_new, v_new, cache_kt, cache_v)


def _merge_ffn_body(*refs, d_ff, seq_rows):
    (x_ref, oa_ref, ob_ref, ga_ref, gb_ref, wa_ref, wb_ref, wo_ref, fg_ref, wup_ref, cw_ref, cb_ref,
     wdn_ref, ng_ref) = refs[:14]
    if seq_rows is None:
        y_ref, conv_ref, acc_ref, carry_ref, shift_ref = refs[14:]
    else:
        h0_ref, h1_ref, y_ref, last2_ref, last1_ref, acc_ref, u_ref = refs[14:]
    tm = x_ref.shape[0]
    assert cw_ref.shape[0] == 3

    mix = (ga_ref[...].astype(F32) * _dot(oa_ref[...], wa_ref[...])
           + gb_ref[...].astype(F32) * _dot(ob_ref[...], wb_ref[...]))
    x1 = x_ref[...] + _dot(mix.astype(BF16), wo_ref[...])
    xn = _rms(x1, fg_ref[...]).astype(BF16)

    if seq_rows is None:
        @pl.when(pl.program_id(1) == 0)
        def _():
            carry_ref[...] = jnp.zeros_like(carry_ref)
    else:
        pos = _mod_pow2(lax.broadcasted_iota(jnp.int32, (tm, FFN_COLS), 0), seq_rows)

    def taps(u, u1, u2, cols):
        return cb_ref[:, cols] + cw_ref[0:1, cols] * u2 + cw_ref[1:2, cols] * u1 + cw_ref[2:3, cols] * u

    def conv(u, cols):
        if seq_rows is None:
            stage = shift_ref.at[cols.start // d_ff]
            stage[0:8, :] = carry_ref[:, cols]
            stage[8:8 + tm, :] = u
            carry_ref[:, cols] = u[tm - 8:, :]
            conv_ref[:, cols] = u[tm - 8:, :]
            return taps(u, stage[7:7 + tm, :], stage[6:6 + tm, :], cols)
        else:
            n_seq = tm // seq_rows
            expand = lambda h: jnp.broadcast_to(h[:, None, :], (n_seq, seq_rows, FFN_COLS)).reshape(tm, FFN_COLS)
            prev2 = expand(h0_ref[:, cols])
            prev1 = expand(h1_ref[:, cols])
            for part in range(FFN_COLS // LANES):
                lanes = _lanes(part)
                out = slice(cols.start + part * LANES, cols.start + (part + 1) * LANES)
                u_ref[...] = u[:, lanes]
                last2_ref[:, out] = u_ref[pl.ds(seq_rows - 2, n_seq, stride=seq_rows), :]
                last1_ref[:, out] = u_ref[pl.ds(seq_rows - 1, n_seq, stride=seq_rows), :]
            u1 = jnp.where(pos == 0, prev1, pltpu.roll(u, 1, 0))
            u2 = jnp.where(pos == 0, prev2, jnp.where(pos == 1, prev1, pltpu.roll(u, 2, 0)))
            return taps(u, u1, u2, cols)

    gate_cols = lambda c: slice(c * FFN_COLS, (c + 1) * FFN_COLS)
    val_cols = lambda c: slice(d_ff + c * FFN_COLS, d_ff + (c + 1) * FFN_COLS)
    ups, hidden = {}, {}

    def up(c):
        ups[c] = _dot(xn, wup_ref[:, gate_cols(c)]), _dot(xn, wup_ref[:, val_cols(c)])

    def gated(c):
        ug, uv = ups.pop(c)
        gate = conv(ug, gate_cols(c))
        hidden[c] = (gate * conv(uv, val_cols(c)) / (1.0 + _exp(gate, -1.0))).astype(BF16)

    def down(c):
        part = _dot(hidden.pop(c), wdn_ref[gate_cols(c), :])
        if c == 0:
            acc_ref[...] = part
        else:
            acc_ref[...] += part

    _staggered([up, gated, down], d_ff // FFN_COLS)
    y_ref[...] = _rms(x1 + acc_ref[...], ng_ref[...])


def _merge_ffn(x2d, oa, ob, ga, gb, wa, wb, wo, fg, wup, cw, cb, wdn, ng, *, batch, history=None):
    n, d_model = x2d.shape
    width = oa.shape[1]
    d_ff = wdn.shape[0]
    two_ff = wup.shape[1]
    seq = n // batch
    consts = [wa, wb, wo, fg, wup, cw, cb, wdn, ng]
    const_specs = [_const_spec(a.shape) for a in consts]
    if history is None:
        tm = min(FFN_ROWS, seq)
        nt = seq // tm
        grid = (batch, nt)
        row = lambda b, j: (b * nt + j, 0)
        extra, extra_specs = [], []
        conv_shape = [jax.ShapeDtypeStruct((batch, 8, two_ff), F32)]
        conv_spec = [pl.BlockSpec((None, 8, two_ff), lambda b, j: (b, 0, 0))]
        scratch = [pltpu.VMEM((tm, d_model), F32), pltpu.VMEM((8, two_ff), F32),
                   pltpu.VMEM((2, tm + 8, FFN_COLS), F32)]
        seq_rows = None
    else:
        tm = n
        grid = (1,)
        row = lambda i: (0, 0)
        extra = list(history)
        extra_specs = [_const_spec(h.shape) for h in history]
        conv_shape = [jax.ShapeDtypeStruct((batch, two_ff), F32)] * 2
        conv_spec = [pl.BlockSpec((batch, two_ff), row)] * 2
        scratch = [pltpu.VMEM((tm, d_model), F32), pltpu.VMEM((tm, LANES), F32)]
        seq_rows = seq
    tile = lambda cols: pl.BlockSpec((tm, cols), row)
    return pl.pallas_call(
        functools.partial(_merge_ffn_body, d_ff=d_ff, seq_rows=seq_rows),
        grid=grid,
        in_specs=[tile(d_model), tile(width), tile(width), tile(d_model), tile(d_model)]
                 + const_specs + extra_specs,
        out_specs=[tile(d_model)] + conv_spec,
        out_shape=[jax.ShapeDtypeStruct((n, d_model), F32)] + conv_shape,
        scratch_shapes=scratch,
        compiler_params=_params(len(grid)),
        name="merge_ffn" if history is None else "merge_ffn_streams",
    )(x2d, oa, ob, ga, gb, *consts, *extra)


def kernel(x_prompt, x_sample, cache_diff_k, cache_diff_v, cache_sb_k, cache_sb_v, state_conv,
           attn_norm_g, w_in, lambda_q1, lambda_k1, lambda_q2, lambda_k2, subln_g, w_branch_a,
           w_branch_b, w_out, ffn_norm_g, w_up, conv_w, conv_b, w_down, final_norm_g):
    depth = w_in.shape[0]
    assert depth == 1, "single-layer trunk only"
    batch, seq, d_model = x_prompt.shape
    dec_batch, dec_seq, _ = x_sample.shape
    past = cache_diff_k.shape[2]
    width = d_model // 2
    da_heads, _, da_dim = cache_diff_k.shape[3:]
    da_v = cache_diff_v.shape[-1]
    sb_heads, sb_dim = cache_sb_k.shape[3:]
    assert da_dim == sb_dim and 2 * da_dim == LANES and da_v == LANES
    half = da_dim

    bf = lambda w: w[0].astype(BF16)
    row1 = lambda v: v.reshape(1, -1)
    w_in_bf = bf(w_in)
    consts = (bf(w_branch_a), bf(w_branch_b), bf(w_out), ffn_norm_g, bf(w_up), conv_w[0], conv_b,
              bf(w_down), row1(final_norm_g))
    lams = [row1(v[0]) for v in (lambda_q1, lambda_k1, lambda_q2, lambda_k2)]
    g_row, g_col = row1(subln_g[0]), subln_g[0].reshape(-1, 1)
    specs = lambda arrays: [_const_spec(a.shape) for a in arrays]

    xp = x_prompt.reshape(batch * seq, d_model)
    cos_p, sin_p = _rope_tables(jnp.arange(seq), da_dim)
    (p_dk, p_dv, p_sk, p_sv, qd, kd, vd, qs, ks, vs, ga, gb) = _in_proj(
        xp, attn_norm_g, w_in_bf, cos_p, sin_p, head_dim=da_dim, seq=seq)
    major = lambda c, *dims: jnp.moveaxis(c.reshape(batch, *dims, seq), -1, 1)
    p_dk, p_sk, p_sv = (major(p_dk, da_heads, 2, da_dim), major(p_sk, sb_heads, sb_dim),
                        major(p_sv, sb_heads, sb_dim))
    oa = _prompt_attention(_diff_prompt_body, "diff_prompt", lams + [g_col], specs(lams + [g_col]), qd, kd, vd,
                           batch=batch, seq=seq, rows=DIFF_ROWS, half=half)
    ob = _prompt_attention(_stick_prompt_body, "stick_prompt", [], [], qs, ks, vs,
                           batch=batch, seq=seq, rows=STICK_ROWS, half=half)
    y_p, conv_p = _merge_ffn(xp, oa, ob, ga, gb, *consts, batch=batch)

    xs = x_sample.reshape(dec_batch * dec_seq, d_model)
    cos_s, sin_s = _rope_tables(past + jnp.arange(dec_seq), da_dim)
    cos_s, sin_s = jnp.tile(cos_s, (dec_batch, 1)), jnp.tile(sin_s, (dec_batch, 1))
    (s_dk, s_dv, s_sk, s_sv, qd, kd, vd, qs, ks, vs, ga, gb) = _in_proj(
        xs, attn_norm_g, w_in_bf, cos_s, sin_s, head_dim=da_dim)
    minor = lambda c: jnp.moveaxis(c[0], 1, -1).reshape(dec_batch, width, past)
    oa = _decode_attention(functools.partial(_diff_decode_body, rows=dec_seq, half=half), "diff_decode",
                           lams + [g_row], specs(lams + [g_row]), qd, kd, vd,
                           minor(cache_diff_k), cache_diff_v[0].reshape(dec_batch, past * da_heads, da_v),
                           rows=dec_seq)
    ob = _decode_attention(functools.partial(_stick_decode_body, rows=dec_seq, half=half), "stick_decode",
                           [], [], qs, ks, vs, minor(cache_sb_k), minor(cache_sb_v), rows=dec_seq)
    y_s, last2, last1 = _merge_ffn(xs, oa, ob, ga, gb, *consts, batch=dec_batch,
                                   history=(state_conv[0, :, 0], state_conv[0, :, 1]))

    conv_p = conv_p[:, 6:]
    conv_s = jnp.stack([last2, last1], axis=1)
    return (y_p.reshape(batch, seq, d_model), y_s.reshape(dec_batch, dec_seq, d_model),
            p_dk[None], p_dv.reshape(depth, batch, seq, da_heads, da_v), p_sk[None], p_sv[None],
            conv_p[None],
            s_dk.reshape(depth, dec_batch, dec_seq, da_heads, 2, da_dim),
            s_dv.reshape(depth, dec_batch, dec_seq, da_heads, da_v),
            s_sk.reshape(depth, dec_batch, dec_seq, sb_heads, sb_dim),
            s_sv.reshape(depth, dec_batch, dec_seq, sb_heads, sb_dim),
            conv_s[None])
```

```python
import functools

import jax
import jax.numpy as jnp
from jax import lax
from jax.experimental import pallas as pl
from jax.experimental.pallas import tpu as pltpu

F32 = jnp.float32
BF16 = jnp.bfloat16

CHUNK = 64
ROPE_THETA = 10000.0
NORM_EPS = 1e-6
NEG_INF = -1e30
LAMBDA_INIT_LAYER0 = 0.8 - 0.6 * 1.0
LOG2_E = 1.4426950408889634
DEAD_SOFTPLUS_SUM = 104.0

LANES = 128
VMEM_LIMIT_BYTES = 56 * 1024 * 1024

PROJ_ROWS = 512
DIFF_ROWS = 512
STICK_ROWS = 256
FFN_ROWS = 512
FFN_COLS = 256
FFN_BLOCKS = 2
DECODE_KEYS = 256


def _rms(x, g):
    return x * lax.rsqrt(jnp.mean(x * x, axis=-1, keepdims=True) + NORM_EPS) * g


def _dot(a, b):
    return jnp.dot(a, b, preferred_element_type=F32)


def _dot_nt(a, b):
    return lax.dot_general(a, b, (((1,), (1,)), ((), ())), preferred_element_type=F32)


def _log2(n):
    assert n > 0 and n & (n - 1) == 0, n
    return n.bit_length() - 1


def _mod_pow2(x, n):
    _log2(n)
    return jnp.bitwise_and(x, n - 1)


def _div_pow2(x, n):
    return jnp.right_shift(x, _log2(n))


def _lanes(g):
    return slice(g * LANES, (g + 1) * LANES)


def _const_spec(shape):
    return pl.BlockSpec(shape, lambda *_: (0,) * len(shape), pipeline_mode=pl.Buffered(1))


def _params(n_axes, **flags):
    return pltpu.CompilerParams(dimension_semantics=("arbitrary",) * n_axes,
                                vmem_limit_bytes=VMEM_LIMIT_BYTES, flags=flags or None)


def _in_proj_body(x_ref, g_ref, w_ref, cos_ref, sin_ref,
                  kd_ref, vd_ref, ks_ref, vs_ref,
                  qd_bf, kd_bf, vd_bf, qs_bf, ks_bf, vs_bf, ga_ref, gb_ref,
                  *, width, d_model, head_dim, q_scale, position_minor):
    xb = _rms(x_ref[...], g_ref[...]).astype(BF16)

    def proj(c0, n):
        return _dot(xb, w_ref[:, c0:c0 + n])

    def put(ref, c, z):
        if position_minor:
            ref[_lanes(c), :] = z.T
        else:
            ref[:, _lanes(c)] = z

    cos = cos_ref[...]
    sin = sin_ref[...]
    lane = lax.broadcasted_iota(jnp.int32, cos.shape, 1)
    half = head_dim // 2
    first_half = _mod_pow2(lane, head_dim) < half

    def rope(z):
        rot = jnp.where(first_half, pltpu.roll(z, LANES - half, 1), pltpu.roll(z, half, 1))
        return z * cos + rot * sin

    tiles = range(width // LANES)

    def diff_q(z):
        for c in tiles:
            qd_bf[:, _lanes(c)] = (rope(z[:, _lanes(c)]) * q_scale).astype(BF16)

    def diff_k(z):
        for c in tiles:
            rk = rope(z[:, _lanes(c)])
            put(kd_ref, c, rk)
            kd_bf[:, _lanes(c)] = rk.astype(BF16)

    def diff_v(z):
        vd_bf[...] = z.astype(BF16)
        if position_minor:
            for h in tiles:
                vd_ref[pl.ds(h, z.shape[0], stride=len(tiles)), :] = z[:, _lanes(h)]
        else:
            vd_ref[...] = z

    def stick_q(z):
        qs_bf[...] = (z * q_scale).astype(BF16)

    def stick_kv(out_ref, bf_ref):
        def post(z):
            bf_ref[...] = z.astype(BF16)
            for c in tiles:
                put(out_ref, c, z[:, _lanes(c)])
        return post

    def gate(out_ref, c):
        def post(z):
            out_ref[:, c * width:(c + 1) * width] = (1.0 / (1.0 + _exp(z, -1.0))).astype(BF16)
        return post

    posts = [diff_q, diff_k, diff_v, stick_q, stick_kv(ks_ref, ks_bf), stick_kv(vs_ref, vs_bf)]
    posts += [gate(ref, c) for ref in (ga_ref, gb_ref) for c in range(d_model // width)]
    z = {}

    def matmul(u):
        z[u] = proj(u * width, width)

    _staggered([matmul, lambda u: posts[u](z.pop(u))], len(posts))


def _in_proj(x2d, g, w_bf, cos_t, sin_t, *, head_dim, seq=None):
    n, d_model = x2d.shape
    n_in = w_bf.shape[1]
    width = d_model // 2
    tm = min(PROJ_ROWS, n)
    n_tab = cos_t.shape[0] // tm
    row = lambda i: (i, 0)
    tab = lambda i: (i % n_tab, 0)
    row_spec = pl.BlockSpec((tm, width), row)
    bf_out = jax.ShapeDtypeStruct((n, width), BF16)
    gate_out = jax.ShapeDtypeStruct((n, d_model), BF16)
    if seq is None:
        cache_specs = [row_spec] * 4
        cache_shapes = [jax.ShapeDtypeStruct((n, width), F32)] * 4
    else:
        nt = seq // tm
        minor_spec = pl.BlockSpec((None, width, tm), lambda i: (i // nt, 0, i % nt))
        minor_shape = jax.ShapeDtypeStruct((n // seq, width, seq), F32)
        heads = width // LANES
        cache_specs = [minor_spec, pl.BlockSpec((tm * heads, LANES), row), minor_spec, minor_spec]
        cache_shapes = [minor_shape, jax.ShapeDtypeStruct((n * heads, LANES), F32), minor_shape, minor_shape]
    body = functools.partial(_in_proj_body, width=width, d_model=d_model, head_dim=head_dim,
                             q_scale=head_dim ** -0.5, position_minor=seq is not None)
    return pl.pallas_call(
        body,
        grid=(n // tm,),
        in_specs=[pl.BlockSpec((tm, d_model), row),
                  _const_spec((1, d_model)),
                  _const_spec((d_model, n_in)),
                  pl.BlockSpec((tm, LANES), tab),
                  pl.BlockSpec((tm, LANES), tab)],
        out_specs=cache_specs + [row_spec] * 6 + [pl.BlockSpec((tm, d_model), row)] * 2,
        out_shape=cache_shapes + [bf_out] * 6 + [gate_out] * 2,
        compiler_params=_params(1),
        name="in_proj",
    )(x2d, g, w_bf, cos_t, sin_t)


def _rope_tables(pos, head_dim):
    half = head_dim // 2
    inv = ROPE_THETA ** (-jnp.arange(half, dtype=F32) * 2.0 / head_dim)
    ang = pos.astype(F32)[:, None] * inv[None, :]
    cos = jnp.cos(ang)
    sin = jnp.sin(ang)
    reps = LANES // head_dim
    return (jnp.tile(jnp.concatenate([cos, cos], -1), (1, reps)),
            jnp.tile(jnp.concatenate([-sin, sin], -1), (1, reps)))


def _stack_halves(q, half):
    lane = lax.broadcasted_iota(jnp.int32, q.shape, 1)
    zero = jnp.zeros_like(q)
    return jnp.concatenate([jnp.where(lane < half, q, zero), jnp.where(lane >= half, q, zero)], axis=0)


def _exp(x, sign=1.0):
    return jnp.exp2(x * (sign * LOG2_E))


def _softplus_parts(z):
    sp = jnp.maximum(z, 0.0) + jnp.log(1.0 + _exp(jnp.abs(z), -1.0))
    return sp, z - sp


def _split_bf16(x):
    hi = x.astype(BF16)
    return hi, (x - hi.astype(F32)).astype(BF16)


def _suffix_matrix(n, keys_on_rows):
    a = lax.broadcasted_iota(jnp.int32, (n, n), 0)
    b = lax.broadcasted_iota(jnp.int32, (n, n), 1)
    after = (b > a) if keys_on_rows else (a > b)
    return jnp.where(after, 1.0, 0.0).astype(BF16)


def _lambda_value(lq1, lk1, lq2, lk2):
    return (jnp.exp(jnp.sum(lq1[...] * lk1[...], axis=-1, keepdims=True))
            - jnp.exp(jnp.sum(lq2[...] * lk2[...], axis=-1, keepdims=True))
            + LAMBDA_INIT_LAYER0)


def _staggered(stages, n, order=None):
    order = order or [0] + list(range(len(stages) - 1, 0, -1))
    for t in range(n + len(stages) - 1):
        for s in order:
            if 0 <= t - s < n:
                stages[s](t - s)


def _softmax_weights_t(s, m, l, mask):
    if mask is not None:
        s = jnp.where(mask, s, NEG_INF)
    m_new = jnp.maximum(m, jnp.max(s, axis=0, keepdims=True))
    alpha = _exp(m - m_new)
    p = _exp(s - m_new)
    return m_new, alpha * l + jnp.sum(p, axis=0, keepdims=True), alpha, p.astype(BF16)


def _stick_parts_t(z, mask):
    sp, log_sig = _softplus_parts(z)
    if mask is not None:
        sp = jnp.where(mask, sp, 0.0)
    return log_sig, jnp.sum(sp, axis=0, keepdims=True), _split_bf16(sp)


def _stick_weights_t(log_sig, later, r, mask):
    a = _exp(log_sig - later - r)
    if mask is not None:
        a = jnp.where(mask, a, 0.0)
    return a.astype(BF16)


def _key_query_positions(n_keys, rows):
    s = lax.broadcasted_iota(jnp.int32, (n_keys, 2 * rows), 0)
    t = _mod_pow2(lax.broadcasted_iota(jnp.int32, (n_keys, 2 * rows), 1), rows)
    return s, t


def _diff_finish_t(acc, l, lam, g_col, rows):
    o = acc / l
    o = o[:, :rows] - lam * o[:, rows:]
    y = o * lax.rsqrt(jnp.mean(o * o, axis=0, keepdims=True) + NORM_EPS) * g_col
    return (y * (1.0 - LAMBDA_INIT_LAYER0)).T.astype(BF16)


def _stick_finish_t(acc, rows, half):
    row = lax.broadcasted_iota(jnp.int32, (LANES, rows), 0)
    return jnp.where(row < half, acc[:, :rows], acc[:, rows:]).T.astype(BF16)


def _transpose_values(v_ref, vt_ref):
    nblk, _, n = vt_ref.shape
    for c in range(nblk):
        vt_ref[c] = v_ref[c * n:(c + 1) * n, :].astype(F32).T.astype(BF16)


def _diff_prompt_body(lq1, lk1, lq2, lk2, g_ref, q_ref, k_ref, v_ref, o_ref, vt_ref, acc_ref, *, rows, half):
    i = pl.program_id(1)
    groups = q_ref.shape[1] // LANES

    @pl.when(i == 0)
    def _():
        _transpose_values(v_ref, vt_ref)

    qs = [_stack_halves(q_ref[:, _lanes(g)], half) for g in range(groups)]
    acc_ref[...] = jnp.zeros_like(acc_ref)

    def step(j, stats, mask):
        off = pl.multiple_of(j * rows, rows)
        scores, weights, out = {}, {}, {}

        def score(g):
            scores[g] = _dot_nt(k_ref[pl.ds(off, rows), _lanes(g)], qs[g])

        def weigh(g):
            m, l, alpha, p = _softmax_weights_t(scores.pop(g), *stats[g], mask)
            out[g] = (m, l)
            weights[g] = (alpha, p)

        def update(g):
            alpha, p = weights.pop(g)
            acc_ref[g] = alpha * acc_ref[g] + _dot(vt_ref[j, _lanes(g), :], p)

        _staggered([score, weigh, update], groups)
        return tuple(out[g] for g in range(groups))

    init = tuple((jnp.full((1, 2 * rows), NEG_INF, F32), jnp.zeros((1, 2 * rows), F32))
                 for _ in range(groups))
    stats = lax.fori_loop(0, i, lambda j, st: step(j, st, None), init)
    s, t = _key_query_positions(rows, rows)
    stats = step(i, stats, _div_pow2(s, CHUNK) <= _div_pow2(t, CHUNK))
    lam = _lambda_value(lq1, lk1, lq2, lk2)
    for g in range(groups):
        o_ref[:, _lanes(g)] = _diff_finish_t(acc_ref[g], stats[g][1], lam, g_ref[...], rows)


def _stick_prompt_body(q_ref, k_ref, v_ref, o_ref, vt_ref, acc_ref, *, rows, half):
    i = pl.program_id(1)
    groups = q_ref.shape[1] // LANES

    @pl.when(i == 0)
    def _():
        _transpose_values(v_ref, vt_ref)

    qs = [_stack_halves(q_ref[:, _lanes(g)], half) for g in range(groups)]
    suffix = _suffix_matrix(rows, keys_on_rows=True)
    acc_ref[...] = jnp.zeros_like(acc_ref)

    def step(j, carry, mask):
        off = pl.multiple_of(j * rows, rows)
        logits, parts, sums, weights, out = {}, {}, {}, {}, {}

        def logit(g):
            logits[g] = _dot_nt(k_ref[pl.ds(off, rows), _lanes(g)], qs[g])

        def split(g):
            log_sig, total, parts[g] = _stick_parts_t(logits.pop(g), mask)
            out[g] = carry[g] + total
            sums[g] = log_sig

        def later(g):
            hi, lo = parts.pop(g)
            sums[g] = (sums[g], _dot(suffix, hi) + _dot(suffix, lo))

        def weigh(g):
            weights[g] = _stick_weights_t(*sums.pop(g), carry[g], mask)

        def update(g):
            acc_ref[g] += _dot(vt_ref[j, _lanes(g), :], weights.pop(g))

        _staggered([logit, split, later, weigh, update], groups)
        return tuple(out[g] for g in range(groups))

    def live(carry):
        return jnp.min(functools.reduce(jnp.minimum, carry)) < DEAD_SOFTPLUS_SUM

    def older_block(state):
        jj, _, carry = state
        carry = step(i - 1 - jj, carry, None)
        return jj + 1, live(carry), carry

    s, t = _key_query_positions(rows, rows)
    carry = step(i, tuple(jnp.zeros((1, 2 * rows), F32) for _ in range(groups)), s < t)
    lax.while_loop(lambda state: (state[0] < i) & state[1], older_block, (jnp.int32(0), live(carry), carry))
    for g in range(groups):
        o_ref[:, _lanes(g)] = _stick_finish_t(acc_ref[g], rows, half)


def _prompt_attention(body, name, extra_inputs, extra_specs, q, k, v, *, batch, seq, rows, half):
    n, width = q.shape
    body = functools.partial(body, rows=rows, half=half)
    nq = seq // rows
    q_spec = pl.BlockSpec((rows, width), lambda b, i: (b * nq + i, 0))
    kv_spec = pl.BlockSpec((seq, width), lambda b, i: (b, 0))
    return pl.pallas_call(
        body,
        grid=(batch, nq),
        in_specs=extra_specs + [q_spec, kv_spec, kv_spec],
        out_specs=q_spec,
        out_shape=jax.ShapeDtypeStruct((n, width), BF16),
        scratch_shapes=[pltpu.VMEM((nq, width, rows), BF16),
                        pltpu.VMEM((width // LANES, LANES, 2 * rows), F32)],
        compiler_params=_params(2),
        name=name,
    )(*extra_inputs, q, k, v)


def _pad_rows(x, n):
    return jnp.concatenate([x, jnp.zeros((n - x.shape[0], x.shape[1]), x.dtype)], axis=0)


def _stacked_positions(rows, n_keys):
    t = _mod_pow2(lax.broadcasted_iota(jnp.int32, (2 * rows, n_keys), 0), rows)
    s = lax.broadcasted_iota(jnp.int32, (2 * rows, n_keys), 1)
    return t, s


def _diff_decode_body(lq1, lk1, lq2, lk2, g_ref, q_ref, kn_ref, vn_ref, ckt_ref, cv_ref, o_ref,
                      *, rows, half):
    past = ckt_ref.shape[1]
    lam = _lambda_value(lq1, lk1, lq2, lk2)
    t, s = _stacked_positions(rows, LANES)
    new_mask = (s < rows) & (_div_pow2(past + s, CHUNK) <= _div_pow2(past + t, CHUNK))
    groups = q_ref.shape[1] // LANES
    scores, weights = {}, {}

    def score(g):
        qs = _stack_halves(q_ref[:, _lanes(g)], half)
        scores[g] = (_dot(qs, ckt_ref[_lanes(g), :].astype(BF16)),
                     _dot_nt(qs, _pad_rows(kn_ref[:, _lanes(g)], LANES)))

    def weigh(g):
        s_c, s_n = scores.pop(g)
        s_n = jnp.where(new_mask, s_n, NEG_INF)
        m = jnp.maximum(jnp.max(s_c, axis=1, keepdims=True), jnp.max(s_n, axis=1, keepdims=True))
        p_c = _exp(s_c - m)
        p_n = _exp(s_n - m)
        l = jnp.sum(p_c, axis=1, keepdims=True) + jnp.sum(p_n, axis=1, keepdims=True)
        weights[g] = (p_c.astype(BF16), p_n.astype(BF16), l)

    def update(g):
        p_c, p_n, l = weights.pop(g)
        v_c = cv_ref[pl.ds(g, past, stride=groups), :]
        o = (_dot(p_c, v_c.astype(BF16)) + _dot(p_n, _pad_rows(vn_ref[:, _lanes(g)], LANES))) / l
        o = o[:rows] - lam * o[rows:]
        o_ref[:, _lanes(g)] = (_rms(o, g_ref[...]) * (1.0 - LAMBDA_INIT_LAYER0)).astype(BF16)

    _staggered([score, weigh, update], groups)


def _stick_decode_body(q_ref, kn_ref, vn_ref, ckt_ref, cvt_ref, o_ref, *, rows, half):
    past = ckt_ref.shape[1]
    suffix = _suffix_matrix(DECODE_KEYS, keys_on_rows=False)
    suffix_new = _suffix_matrix(LANES, keys_on_rows=False)
    t, s = _stacked_positions(rows, LANES)
    new_mask = (s < rows) & (s < t)
    lane = lax.broadcasted_iota(jnp.int32, (rows, LANES), 1)
    blocks = [slice(b * DECODE_KEYS, (b + 1) * DECODE_KEYS) for b in range(past // DECODE_KEYS)]
    logits, parts, laters, weights = {}, {}, {}, {}

    def logit(g):
        qs = _stack_halves(q_ref[:, _lanes(g)], half)
        logits[g] = (_dot_nt(qs, _pad_rows(kn_ref[:, _lanes(g)], LANES)),
                     _dot(qs, ckt_ref[_lanes(g), :].astype(BF16)))

    def split(g):
        z_n, z_c = logits.pop(g)
        sp_n, log_sig_n = _softplus_parts(z_n)
        sp_n = jnp.where(new_mask, sp_n, 0.0)
        sp_c, log_sig_c = _softplus_parts(z_c)
        carries = [jnp.sum(sp_n, axis=1, keepdims=True)]
        for cols in reversed(blocks[1:]):
            carries.append(carries[-1] + jnp.sum(sp_c[:, cols], axis=1, keepdims=True))
        parts[g] = (_split_bf16(sp_n), _split_bf16(sp_c))
        laters[g] = (log_sig_n, log_sig_c, carries[::-1])

    def later(g):
        (hi_n, lo_n), (hi_c, lo_c) = parts.pop(g)
        laters[g] += (_dot(hi_n, suffix_new) + _dot(lo_n, suffix_new),
                      [_dot(hi_c[:, cols], suffix) + _dot(lo_c[:, cols], suffix) for cols in blocks])

    def weigh(g):
        log_sig_n, log_sig_c, carries, later_n, later_c = laters.pop(g)
        a_n = jnp.where(new_mask, _exp(log_sig_n - later_n), 0.0).astype(BF16)
        weights[g] = (a_n, [_exp(log_sig_c[:, cols] - later_c[b] - carries[b]).astype(BF16)
                            for b, cols in enumerate(blocks)])

    def update(g):
        a_n, a_c = weights.pop(g)
        vt = cvt_ref[_lanes(g), :].astype(BF16)
        acc = _dot(a_n, _pad_rows(vn_ref[:, _lanes(g)], LANES))
        for b, cols in enumerate(blocks):
            acc = acc + _dot_nt(a_c[b], vt[:, cols])
        o_ref[:, _lanes(g)] = jnp.where(lane < half, acc[:rows], acc[rows:]).astype(BF16)

    _staggered([logit, split, later, weigh, update], q_ref.shape[1] // LANES)


def _decode_attention(body, name, extra_inputs, extra_specs, q, k_new, v_new, cache_kt, cache_v, *, rows):
    n, width = q.shape
    new_spec = pl.BlockSpec((rows, width), lambda b: (b, 0))
    cache_spec = lambda c: pl.BlockSpec((None,) + c.shape[1:], lambda b: (b,) + (0,) * (c.ndim - 1))
    return pl.pallas_call(
        body,
        grid=(n // rows,),
        in_specs=extra_specs + [new_spec, new_spec, new_spec, cache_spec(cache_kt), cache_spec(cache_v)],
        out_specs=new_spec,
        out_shape=jax.ShapeDtypeStruct((n, width), BF16),
        compiler_params=_params(1),
        name=name,
    )(*extra_inputs, q, k_new, v_new, cache_kt, cache_v)


def _merge_ffn_body(*refs, d_ff, seq_rows):
    (x_ref, oa_ref, ob_ref, ga_ref, gb_ref, wa_ref, wb_ref, wo_ref, fg_ref, wup_ref, cw_ref, cb_ref,
     wdn_ref, ng_ref) = refs[:14]
    if seq_rows is None:
        y_ref, conv_ref, acc_ref, carry_ref = refs[14:]
    else:
        h0_ref, h1_ref, y_ref, last2_ref, last1_ref, acc_ref, u_ref = refs[14:]
    d_model = x_ref.shape[1]
    parts = acc_ref.shape[0]
    tm = x_ref.shape[0] // parts
    block = lambda p: slice(p * tm, (p + 1) * tm)
    assert cw_ref.shape[0] == 3

    if seq_rows is None:
        n_slabs = tm // 8
        interleave = lambda a: jnp.swapaxes(a.reshape(8, n_slabs, d_model), 0, 1).reshape(tm, d_model)
        position_order = lambda a: jnp.swapaxes(a.reshape(n_slabs, 8, d_model), 0, 1).reshape(tm, d_model)

        @pl.when(pl.program_id(1) == 0)
        def _():
            carry_ref[...] = jnp.zeros_like(carry_ref)
        sublane = lax.broadcasted_iota(jnp.int32, (8, FFN_COLS), 0)
    else:
        interleave = position_order = lambda a: a
        pos = _mod_pow2(lax.broadcasted_iota(jnp.int32, (tm, FFN_COLS), 0), seq_rows)

    branches, mixes, projected, x1, xn = {}, {}, {}, {}, {}

    def branch(p):
        branches[p] = _dot(oa_ref[block(p), :], wa_ref[...]), _dot(ob_ref[block(p), :], wb_ref[...])

    def mixed(p):
        bra, brb = branches.pop(p)
        mixes[p] = (ga_ref[block(p), :].astype(F32) * bra + gb_ref[block(p), :].astype(F32) * brb).astype(BF16)

    def project(p):
        projected[p] = _dot(mixes.pop(p), wo_ref[...])

    def residual(p):
        x1[p] = interleave(x_ref[block(p), :] + projected.pop(p))
        xn[p] = _rms(x1[p], fg_ref[...]).astype(BF16)

    def finish(p):
        y_ref[block(p), :] = position_order(_rms(x1[p] + acc_ref[p], ng_ref[...]))

    def taps(u, u1, u2, cols):
        return cb_ref[:, cols] + cw_ref[0:1, cols] * u2 + cw_ref[1:2, cols] * u1 + cw_ref[2:3, cols] * u

    def conv(u, cols):
        if seq_rows is None:
            prev2 = carry_ref[7:8, cols]
            prev1 = carry_ref[15:16, cols]
            carry_ref[:, cols] = u[tm - 16:, :]
            conv_ref[:, cols] = u[tm - 16:, :]
            before1 = jnp.where(sublane == 0, prev1, pltpu.roll(u[tm - 8:, :], 1, 0))
            before2 = jnp.where(sublane == 0, prev2, pltpu.roll(u[tm - 16:tm - 8, :], 1, 0))
            u1 = jnp.concatenate([before1, u[:tm - 8, :]], axis=0)
            u2 = jnp.concatenate([before2, before1, u[:tm - 16, :]], axis=0)
            return taps(u, u1, u2, cols)
        else:
            n_seq = tm // seq_rows
            expand = lambda h: jnp.broadcast_to(h[:, None, :], (n_seq, seq_rows, FFN_COLS)).reshape(tm, FFN_COLS)
            prev2 = expand(h0_ref[:, cols])
            prev1 = expand(h1_ref[:, cols])
            for part in range(FFN_COLS // LANES):
                lanes = _lanes(part)
                out = slice(cols.start + part * LANES, cols.start + (part + 1) * LANES)
                u_ref[...] = u[:, lanes]
                last2_ref[:, out] = u_ref[pl.ds(seq_rows - 2, n_seq, stride=seq_rows), :]
                last1_ref[:, out] = u_ref[pl.ds(seq_rows - 1, n_seq, stride=seq_rows), :]
            u1 = jnp.where(pos == 0, prev1, pltpu.roll(u, 1, 0))
            u2 = jnp.where(pos == 0, prev2, jnp.where(pos == 1, prev1, pltpu.roll(u, 2, 0)))
            return taps(u, u1, u2, cols)

    chunks = d_ff // FFN_COLS
    gate_cols = lambda c: slice(c * FFN_COLS, (c + 1) * FFN_COLS)
    val_cols = lambda c: slice(d_ff + c * FFN_COLS, d_ff + (c + 1) * FFN_COLS)
    ups, hidden = {}, {}

    def up(i):
        p, c = divmod(i, chunks)
        ups[i] = _dot(xn[p], wup_ref[:, gate_cols(c)]), _dot(xn[p], wup_ref[:, val_cols(c)])

    def gated(i):
        c = i % chunks
        ug, uv = ups.pop(i)
        gate = conv(ug, gate_cols(c))
        hidden[i] = (gate * conv(uv, val_cols(c)) / (1.0 + _exp(gate, -1.0))).astype(BF16)

    def down(i):
        p, c = divmod(i, chunks)
        part = _dot(hidden.pop(i), wdn_ref[gate_cols(c), :])
        acc_ref[p] = part if c == 0 else acc_ref[p] + part

    merge = [branch, mixed, project, residual]
    for step in merge:
        step(0)
    for t in range(parts * chunks + 2):
        for stage, fn in ((0, up), (2, down), (1, gated)):
            if 0 <= t - stage < parts * chunks:
                fn(t - stage)
        p, c = divmod(t, chunks)
        if p + 1 < parts and 1 <= c <= len(merge):
            merge[c - 1](p + 1)
        if t >= 2 and (t - 2) % chunks == chunks - 1:
            finish((t - 2) // chunks)


def _merge_ffn(x2d, oa, ob, ga, gb, wa, wb, wo, fg, wup, cw, cb, wdn, ng, *, batch, history=None):
    n, d_model = x2d.shape
    width = oa.shape[1]
    d_ff = wdn.shape[0]
    two_ff = wup.shape[1]
    seq = n // batch
    consts = [wa, wb, wo, fg, wup, cw, cb, wdn, ng]
    const_specs = [_const_spec(a.shape) for a in consts]
    if history is None:
        tm = min(FFN_ROWS, seq)
        nt = seq // tm
        grid = (batch, nt)
        row = lambda b, j: (b * nt + j, 0)
        extra, extra_specs = [], []
        conv_shape = [jax.ShapeDtypeStruct((batch, 16, two_ff), F32)]
        conv_spec = [pl.BlockSpec((None, 16, two_ff), lambda b, j: (b, 0, 0))]
        scratch = [pltpu.VMEM((FFN_BLOCKS, tm // FFN_BLOCKS, d_model), F32), pltpu.VMEM((16, two_ff), F32)]
        seq_rows = None
    else:
        tm = n
        grid = (1,)
        row = lambda i: (0, 0)
        extra = list(history)
        extra_specs = [_const_spec(h.shape) for h in history]
        conv_shape = [jax.ShapeDtypeStruct((batch, two_ff), F32)] * 2
        conv_spec = [pl.BlockSpec((batch, two_ff), row)] * 2
        scratch = [pltpu.VMEM((1, tm, d_model), F32), pltpu.VMEM((tm, LANES), F32)]
        seq_rows = seq
    tile = lambda cols: pl.BlockSpec((tm, cols), row)
    return pl.pallas_call(
        functools.partial(_merge_ffn_body, d_ff=d_ff, seq_rows=seq_rows),
        grid=grid,
        in_specs=[tile(d_model), tile(width), tile(width), tile(d_model), tile(d_model)]
                 + const_specs + extra_specs,
        out_specs=[tile(d_model)] + conv_spec,
        out_shape=[jax.ShapeDtypeStruct((n, d_model), F32)] + conv_shape,
        scratch_shapes=scratch,
        compiler_params=_params(len(grid)),
        name="merge_ffn" if history is None else "merge_ffn_streams",
    )(x2d, oa, ob, ga, gb, *consts, *extra)


def kernel(x_prompt, x_sample, cache_diff_k, cache_diff_v, cache_sb_k, cache_sb_v, state_conv,
           attn_norm_g, w_in, lambda_q1, lambda_k1, lambda_q2, lambda_k2, subln_g, w_branch_a,
           w_branch_b, w_out, ffn_norm_g, w_up, conv_w, conv_b, w_down, final_norm_g):
    depth = w_in.shape[0]
    assert depth == 1, "single-layer trunk only"
    batch, seq, d_model = x_prompt.shape
    dec_batch, dec_seq, _ = x_sample.shape
    past = cache_diff_k.shape[2]
    width = d_model // 2
    da_heads, _, da_dim = cache_diff_k.shape[3:]
    da_v = cache_diff_v.shape[-1]
    sb_heads, sb_dim = cache_sb_k.shape[3:]
    assert da_dim == sb_dim and 2 * da_dim == LANES and da_v == LANES
    half = da_dim

    bf = lambda w: w[0].astype(BF16)
    row1 = lambda v: v.reshape(1, -1)
    w_in_bf = bf(w_in)
    consts = (bf(w_branch_a), bf(w_branch_b), bf(w_out), ffn_norm_g, bf(w_up), conv_w[0], conv_b,
              bf(w_down), row1(final_norm_g))
    lams = [row1(v[0]) for v in (lambda_q1, lambda_k1, lambda_q2, lambda_k2)]
    g_row, g_col = row1(subln_g[0]), subln_g[0].reshape(-1, 1)
    specs = lambda arrays: [_const_spec(a.shape) for a in arrays]

    xp = x_prompt.reshape(batch * seq, d_model)
    cos_p, sin_p = _rope_tables(jnp.arange(seq), da_dim)
    (p_dk, p_dv, p_sk, p_sv, qd, kd, vd, qs, ks, vs, ga, gb) = _in_proj(
        xp, attn_norm_g, w_in_bf, cos_p, sin_p, head_dim=da_dim, seq=seq)
    major = lambda c, *dims: jnp.moveaxis(c.reshape(batch, *dims, seq), -1, 1)
    p_dk, p_sk, p_sv = (major(p_dk, da_heads, 2, da_dim), major(p_sk, sb_heads, sb_dim),
                        major(p_sv, sb_heads, sb_dim))
    oa = _prompt_attention(_diff_prompt_body, "diff_prompt", lams + [g_col], specs(lams + [g_col]), qd, kd, vd,
                           batch=batch, seq=seq, rows=DIFF_ROWS, half=half)
    ob = _prompt_attention(_stick_prompt_body, "stick_prompt", [], [], qs, ks, vs,
                           batch=batch, seq=seq, rows=STICK_ROWS, half=half)
    y_p, conv_p = _merge_ffn(xp, oa, ob, ga, gb, *consts, batch=batch)

    xs = x_sample.reshape(dec_batch * dec_seq, d_model)
    cos_s, sin_s = _rope_tables(past + jnp.arange(dec_seq), da_dim)
    cos_s, sin_s = jnp.tile(cos_s, (dec_batch, 1)), jnp.tile(sin_s, (dec_batch, 1))
    (s_dk, s_dv, s_sk, s_sv, qd, kd, vd, qs, ks, vs, ga, gb) = _in_proj(
        xs, attn_norm_g, w_in_bf, cos_s, sin_s, head_dim=da_dim)
    minor = lambda c: jnp.moveaxis(c[0], 1, -1).reshape(dec_batch, width, past)
    oa = _decode_attention(functools.partial(_diff_decode_body, rows=dec_seq, half=half), "diff_decode",
                           lams + [g_row], specs(lams + [g_row]), qd, kd, vd,
                           minor(cache_diff_k), cache_diff_v[0].reshape(dec_batch, past * da_heads, da_v),
                           rows=dec_seq)
    ob = _decode_attention(functools.partial(_stick_decode_body, rows=dec_seq, half=half), "stick_decode",
                           [], [], qs, ks, vs, minor(cache_sb_k), minor(cache_sb_v), rows=dec_seq)
    y_s, last2, last1 = _merge_ffn(xs, oa, ob, ga, gb, *consts, batch=dec_batch,
                                   history=(state_conv[0, :, 0], state_conv[0, :, 1]))

    conv_p = jnp.stack([conv_p[:, 7], conv_p[:, 15]], axis=1)
    conv_s = jnp.stack([last2, last1], axis=1)
    return (y_p.reshape(batch, seq, d_model), y_s.reshape(dec_batch, dec_seq, d_model),
            p_dk[None], p_dv.reshape(depth, batch, seq, da_heads, da_v), p_sk[None], p_sv[None],
            conv_p[None],
            s_dk.reshape(depth, dec_batch, dec_seq, da_heads, 2, da_dim),
            s_dv.reshape(depth, dec_batch, dec_seq, da_heads, da_v),
            s_sk.reshape(depth, dec_batch, dec_seq, sb_heads, sb_dim),
            s_sv.reshape(depth, dec_batch, dec_seq, sb_heads, sb_dim),
            conv_s[None])
```

```python
import functools

import jax
import jax.numpy as jnp
from jax import lax
from jax.experimental import pallas as pl
from jax.experimental.pallas import tpu as pltpu

F32 = jnp.float32
BF16 = jnp.bfloat16

CHUNK = 64
ROPE_THETA = 10000.0
NORM_EPS = 1e-6
NEG_INF = -1e30
LAMBDA_INIT_LAYER0 = 0.8 - 0.6 * 1.0
LOG2_E = 1.4426950408889634
DEAD_SOFTPLUS_SUM = 104.0

LANES = 128
VMEM_LIMIT_BYTES = 56 * 1024 * 1024

PROJ_ROWS = 512
DIFF_ROWS = 512
STICK_ROWS = 256
ATTN_SEQUENCES = 2
FFN_ROWS = 512
FFN_COLS = 256
FFN_BLOCKS = 2
DECODE_KEYS = 256


def _rms(x, g):
    return x * lax.rsqrt(jnp.mean(x * x, axis=-1, keepdims=True) + NORM_EPS) * g


def _dot(a, b):
    return jnp.dot(a, b, preferred_element_type=F32)


def _dot_nt(a, b):
    return lax.dot_general(a, b, (((1,), (1,)), ((), ())), preferred_element_type=F32)


def _log2(n):
    assert n > 0 and n & (n - 1) == 0, n
    return n.bit_length() - 1


def _mod_pow2(x, n):
    _log2(n)
    return jnp.bitwise_and(x, n - 1)


def _div_pow2(x, n):
    return jnp.right_shift(x, _log2(n))


def _lanes(g):
    return slice(g * LANES, (g + 1) * LANES)


def _const_spec(shape):
    return pl.BlockSpec(shape, lambda *_: (0,) * len(shape), pipeline_mode=pl.Buffered(1))


def _params(n_axes, **flags):
    return pltpu.CompilerParams(dimension_semantics=("arbitrary",) * n_axes,
                                vmem_limit_bytes=VMEM_LIMIT_BYTES, flags=flags or None)


def _in_proj_body(x_ref, g_ref, w_ref, cos_ref, sin_ref,
                  kd_ref, vd_ref, ks_ref, vs_ref,
                  qd_bf, kd_bf, vd_bf, qs_bf, ks_bf, vs_bf, ga_ref, gb_ref,
                  *, width, d_model, head_dim, q_scale, position_minor):
    xb = _rms(x_ref[...], g_ref[...]).astype(BF16)

    def proj(c0, n):
        return _dot(xb, w_ref[:, c0:c0 + n])

    def put(ref, c, z):
        if position_minor:
            ref[_lanes(c), :] = z.T
        else:
            ref[:, _lanes(c)] = z

    cos = cos_ref[...]
    sin = sin_ref[...]
    lane = lax.broadcasted_iota(jnp.int32, cos.shape, 1)
    half = head_dim // 2
    first_half = _mod_pow2(lane, head_dim) < half

    def rope(z):
        rot = jnp.where(first_half, pltpu.roll(z, LANES - half, 1), pltpu.roll(z, half, 1))
        return z * cos + rot * sin

    tiles = range(width // LANES)

    def diff_q(z):
        for c in tiles:
            qd_bf[:, _lanes(c)] = (rope(z[:, _lanes(c)]) * q_scale).astype(BF16)

    def diff_k(z):
        for c in tiles:
            rk = rope(z[:, _lanes(c)])
            put(kd_ref, c, rk)
            kd_bf[:, _lanes(c)] = rk.astype(BF16)

    def diff_v(z):
        vd_bf[...] = z.astype(BF16)
        if position_minor:
            for h in tiles:
                vd_ref[pl.ds(h, z.shape[0], stride=len(tiles)), :] = z[:, _lanes(h)]
        else:
            vd_ref[...] = z

    def stick_q(z):
        qs_bf[...] = (z * q_scale).astype(BF16)

    def stick_kv(out_ref, bf_ref):
        def post(z):
            bf_ref[...] = z.astype(BF16)
            for c in tiles:
                put(out_ref, c, z[:, _lanes(c)])
        return post

    def gate(out_ref, c):
        def post(z):
            out_ref[:, c * width:(c + 1) * width] = (1.0 / (1.0 + _exp(z, -1.0))).astype(BF16)
        return post

    posts = [diff_q, diff_k, diff_v, stick_q, stick_kv(ks_ref, ks_bf), stick_kv(vs_ref, vs_bf)]
    posts += [gate(ref, c) for ref in (ga_ref, gb_ref) for c in range(d_model // width)]
    z = {}

    def matmul(u):
        z[u] = proj(u * width, width)

    _staggered([matmul, lambda u: posts[u](z.pop(u))], len(posts))


def _in_proj(x2d, g, w_bf, cos_t, sin_t, *, head_dim, seq=None):
    n, d_model = x2d.shape
    n_in = w_bf.shape[1]
    width = d_model // 2
    tm = min(PROJ_ROWS, n)
    n_tab = cos_t.shape[0] // tm
    row = lambda i: (i, 0)
    tab = lambda i: (i % n_tab, 0)
    row_spec = pl.BlockSpec((tm, width), row)
    bf_out = jax.ShapeDtypeStruct((n, width), BF16)
    gate_out = jax.ShapeDtypeStruct((n, d_model), BF16)
    if seq is None:
        cache_specs = [row_spec] * 4
        cache_shapes = [jax.ShapeDtypeStruct((n, width), F32)] * 4
    else:
        nt = seq // tm
        minor_spec = pl.BlockSpec((None, width, tm), lambda i: (i // nt, 0, i % nt))
        minor_shape = jax.ShapeDtypeStruct((n // seq, width, seq), F32)
        heads = width // LANES
        cache_specs = [minor_spec, pl.BlockSpec((tm * heads, LANES), row), minor_spec, minor_spec]
        cache_shapes = [minor_shape, jax.ShapeDtypeStruct((n * heads, LANES), F32), minor_shape, minor_shape]
    body = functools.partial(_in_proj_body, width=width, d_model=d_model, head_dim=head_dim,
                             q_scale=head_dim ** -0.5, position_minor=seq is not None)
    return pl.pallas_call(
        body,
        grid=(n // tm,),
        in_specs=[pl.BlockSpec((tm, d_model), row),
                  _const_spec((1, d_model)),
                  _const_spec((d_model, n_in)),
                  pl.BlockSpec((tm, LANES), tab),
                  pl.BlockSpec((tm, LANES), tab)],
        out_specs=cache_specs + [row_spec] * 6 + [pl.BlockSpec((tm, d_model), row)] * 2,
        out_shape=cache_shapes + [bf_out] * 6 + [gate_out] * 2,
        compiler_params=_params(1),
        name="in_proj",
    )(x2d, g, w_bf, cos_t, sin_t)


def _rope_tables(pos, head_dim):
    half = head_dim // 2
    inv = ROPE_THETA ** (-jnp.arange(half, dtype=F32) * 2.0 / head_dim)
    ang = pos.astype(F32)[:, None] * inv[None, :]
    cos = jnp.cos(ang)
    sin = jnp.sin(ang)
    reps = LANES // head_dim
    return (jnp.tile(jnp.concatenate([cos, cos], -1), (1, reps)),
            jnp.tile(jnp.concatenate([-sin, sin], -1), (1, reps)))


def _stack_halves(q, half):
    lane = lax.broadcasted_iota(jnp.int32, q.shape, 1)
    zero = jnp.zeros_like(q)
    return jnp.concatenate([jnp.where(lane < half, q, zero), jnp.where(lane >= half, q, zero)], axis=0)


def _exp(x, sign=1.0):
    return jnp.exp2(x * (sign * LOG2_E))


def _softplus_parts(z):
    sp = jnp.maximum(z, 0.0) + jnp.log(1.0 + _exp(jnp.abs(z), -1.0))
    return sp, z - sp


def _split_bf16(x):
    hi = x.astype(BF16)
    return hi, (x - hi.astype(F32)).astype(BF16)


def _suffix_matrix(n, keys_on_rows):
    a = lax.broadcasted_iota(jnp.int32, (n, n), 0)
    b = lax.broadcasted_iota(jnp.int32, (n, n), 1)
    after = (b > a) if keys_on_rows else (a > b)
    return jnp.where(after, 1.0, 0.0).astype(BF16)


def _lambda_value(lq1, lk1, lq2, lk2):
    return (jnp.exp(jnp.sum(lq1[...] * lk1[...], axis=-1, keepdims=True))
            - jnp.exp(jnp.sum(lq2[...] * lk2[...], axis=-1, keepdims=True))
            + LAMBDA_INIT_LAYER0)


def _staggered(stages, n, order=None):
    order = order or [0] + list(range(len(stages) - 1, 0, -1))
    for t in range(n + len(stages) - 1):
        for s in order:
            if 0 <= t - s < n:
                stages[s](t - s)


def _softmax_weights_t(s, m, l, mask):
    if mask is not None:
        s = jnp.where(mask, s, NEG_INF)
    m_new = jnp.maximum(m, jnp.max(s, axis=0, keepdims=True))
    alpha = _exp(m - m_new)
    p = _exp(s - m_new)
    return m_new, alpha * l + jnp.sum(p, axis=0, keepdims=True), alpha, p.astype(BF16)


def _stick_parts_t(z, mask):
    sp, log_sig = _softplus_parts(z)
    if mask is not None:
        sp = jnp.where(mask, sp, 0.0)
    return log_sig, jnp.sum(sp, axis=0, keepdims=True), _split_bf16(sp)


def _stick_weights_t(log_sig, later, r, mask):
    a = _exp(log_sig - later - r)
    if mask is not None:
        a = jnp.where(mask, a, 0.0)
    return a.astype(BF16)


def _key_query_positions(n_keys, rows):
    s = lax.broadcasted_iota(jnp.int32, (n_keys, 2 * rows), 0)
    t = _mod_pow2(lax.broadcasted_iota(jnp.int32, (n_keys, 2 * rows), 1), rows)
    return s, t


def _diff_finish_t(acc, l, lam, g_col, rows):
    o = acc / l
    o = o[:, :rows] - lam * o[:, rows:]
    y = o * lax.rsqrt(jnp.mean(o * o, axis=0, keepdims=True) + NORM_EPS) * g_col
    return (y * (1.0 - LAMBDA_INIT_LAYER0)).T.astype(BF16)


def _stick_finish_t(acc, rows, half):
    row = lax.broadcasted_iota(jnp.int32, (LANES, rows), 0)
    return jnp.where(row < half, acc[:, :rows], acc[:, rows:]).T.astype(BF16)


def _attention_units(q_ref):
    return [(seq, g) for seq in range(q_ref.shape[0]) for g in range(q_ref.shape[2] // LANES)]


def _transpose_values(v_ref, vt_ref):
    nblk, _, n = vt_ref.shape
    for c in range(nblk):
        vt_ref[c] = v_ref[c * n:(c + 1) * n, :].astype(F32).T.astype(BF16)


def _diff_prompt_body(lq1, lk1, lq2, lk2, g_ref, q_ref, k_ref, v_ref, o_ref, vt_ref, acc_ref, *, rows, half):
    i = pl.program_id(1)
    units = _attention_units(q_ref)

    @pl.when(i == 0)
    def _():
        for seq in range(q_ref.shape[0]):
            _transpose_values(v_ref.at[seq], vt_ref.at[seq])

    qs = [_stack_halves(q_ref[seq, :, _lanes(g)], half) for seq, g in units]
    acc_ref[...] = jnp.zeros_like(acc_ref)

    def step(j, stats, mask):
        off = pl.multiple_of(j * rows, rows)
        scores, weights, out = {}, {}, {}

        def score(u):
            seq, g = units[u]
            scores[u] = _dot_nt(k_ref[seq, pl.ds(off, rows), _lanes(g)], qs[u])

        def weigh(u):
            m, l, alpha, p = _softmax_weights_t(scores.pop(u), *stats[u], mask)
            out[u] = (m, l)
            weights[u] = (alpha, p)

        def update(u):
            seq, g = units[u]
            alpha, p = weights.pop(u)
            acc_ref[u] = alpha * acc_ref[u] + _dot(vt_ref[seq, j, _lanes(g), :], p)

        _staggered([score, weigh, update], len(units))
        return tuple(out[u] for u in range(len(units)))

    init = tuple((jnp.full((1, 2 * rows), NEG_INF, F32), jnp.zeros((1, 2 * rows), F32)) for _ in units)
    stats = lax.fori_loop(0, i, lambda j, st: step(j, st, None), init)
    s, t = _key_query_positions(rows, rows)
    stats = step(i, stats, _div_pow2(s, CHUNK) <= _div_pow2(t, CHUNK))
    lam = _lambda_value(lq1, lk1, lq2, lk2)
    for u, (seq, g) in enumerate(units):
        o_ref[seq, :, _lanes(g)] = _diff_finish_t(acc_ref[u], stats[u][1], lam, g_ref[...], rows)


def _stick_prompt_body(q_ref, k_ref, v_ref, o_ref, vt_ref, acc_ref, *, rows, half):
    i = pl.program_id(1)
    units = _attention_units(q_ref)

    @pl.when(i == 0)
    def _():
        for seq in range(q_ref.shape[0]):
            _transpose_values(v_ref.at[seq], vt_ref.at[seq])

    qs = [_stack_halves(q_ref[seq, :, _lanes(g)], half) for seq, g in units]
    suffix = _suffix_matrix(rows, keys_on_rows=True)
    acc_ref[...] = jnp.zeros_like(acc_ref)

    def step(j, carry, mask):
        off = pl.multiple_of(j * rows, rows)
        logits, parts, sums, weights, out = {}, {}, {}, {}, {}

        def logit(u):
            seq, g = units[u]
            logits[u] = _dot_nt(k_ref[seq, pl.ds(off, rows), _lanes(g)], qs[u])

        def split(u):
            log_sig, total, parts[u] = _stick_parts_t(logits.pop(u), mask)
            out[u] = carry[u] + total
            sums[u] = log_sig

        def later(u):
            hi, lo = parts.pop(u)
            sums[u] = (sums[u], _dot(suffix, hi) + _dot(suffix, lo))

        def weigh(u):
            weights[u] = _stick_weights_t(*sums.pop(u), carry[u], mask)

        def update(u):
            seq, g = units[u]
            acc_ref[u] += _dot(vt_ref[seq, j, _lanes(g), :], weights.pop(u))

        _staggered([logit, split, later, weigh, update], len(units))
        return tuple(out[u] for u in range(len(units)))

    def live(carry):
        return jnp.min(functools.reduce(jnp.minimum, carry)) < DEAD_SOFTPLUS_SUM

    def older_block(state):
        jj, _, carry = state
        carry = step(i - 1 - jj, carry, None)
        return jj + 1, live(carry), carry

    s, t = _key_query_positions(rows, rows)
    carry = step(i, tuple(jnp.zeros((1, 2 * rows), F32) for _ in units), s < t)
    lax.while_loop(lambda state: (state[0] < i) & state[1], older_block, (jnp.int32(0), live(carry), carry))
    for u, (seq, g) in enumerate(units):
        o_ref[seq, :, _lanes(g)] = _stick_finish_t(acc_ref[u], rows, half)


def _prompt_attention(body, name, extra_inputs, extra_specs, q, k, v, *, batch, seq, rows, half):
    n, width = q.shape
    body = functools.partial(body, rows=rows, half=half)
    nq = seq // rows
    together = ATTN_SEQUENCES * STICK_ROWS // rows
    by_seq = lambda a: a.reshape(batch, seq, width)
    q_spec = pl.BlockSpec((together, rows, width), lambda b, i: (b, i, 0))
    kv_spec = pl.BlockSpec((together, seq, width), lambda b, i: (b, 0, 0))
    return pl.pallas_call(
        body,
        grid=(batch // together, nq),
        in_specs=extra_specs + [q_spec, kv_spec, kv_spec],
        out_specs=q_spec,
        out_shape=jax.ShapeDtypeStruct((batch, seq, width), BF16),
        scratch_shapes=[pltpu.VMEM((together, nq, width, rows), BF16),
                        pltpu.VMEM((together * (width // LANES), LANES, 2 * rows), F32)],
        compiler_params=_params(2),
        name=name,
    )(*extra_inputs, by_seq(q), by_seq(k), by_seq(v)).reshape(n, width)


def _pad_rows(x, n):
    return jnp.concatenate([x, jnp.zeros((n - x.shape[0], x.shape[1]), x.dtype)], axis=0)


def _stacked_positions(rows, n_keys):
    t = _mod_pow2(lax.broadcasted_iota(jnp.int32, (2 * rows, n_keys), 0), rows)
    s = lax.broadcasted_iota(jnp.int32, (2 * rows, n_keys), 1)
    return t, s


def _diff_decode_body(lq1, lk1, lq2, lk2, g_ref, q_ref, kn_ref, vn_ref, ckt_ref, cv_ref, o_ref,
                      *, rows, half):
    past = ckt_ref.shape[1]
    lam = _lambda_value(lq1, lk1, lq2, lk2)
    t, s = _stacked_positions(rows, LANES)
    new_mask = (s < rows) & (_div_pow2(past + s, CHUNK) <= _div_pow2(past + t, CHUNK))
    groups = q_ref.shape[1] // LANES
    scores, weights = {}, {}

    def score(g):
        qs = _stack_halves(q_ref[:, _lanes(g)], half)
        scores[g] = (_dot(qs, ckt_ref[_lanes(g), :].astype(BF16)),
                     _dot_nt(qs, _pad_rows(kn_ref[:, _lanes(g)], LANES)))

    def weigh(g):
        s_c, s_n = scores.pop(g)
        s_n = jnp.where(new_mask, s_n, NEG_INF)
        m = jnp.maximum(jnp.max(s_c, axis=1, keepdims=True), jnp.max(s_n, axis=1, keepdims=True))
        p_c = _exp(s_c - m)
        p_n = _exp(s_n - m)
        l = jnp.sum(p_c, axis=1, keepdims=True) + jnp.sum(p_n, axis=1, keepdims=True)
        weights[g] = (p_c.astype(BF16), p_n.astype(BF16), l)

    def update(g):
        p_c, p_n, l = weights.pop(g)
        v_c = cv_ref[pl.ds(g, past, stride=groups), :]
        o = (_dot(p_c, v_c.astype(BF16)) + _dot(p_n, _pad_rows(vn_ref[:, _lanes(g)], LANES))) / l
        o = o[:rows] - lam * o[rows:]
        o_ref[:, _lanes(g)] = (_rms(o, g_ref[...]) * (1.0 - LAMBDA_INIT_LAYER0)).astype(BF16)

    _staggered([score, weigh, update], groups)


def _stick_decode_body(q_ref, kn_ref, vn_ref, ckt_ref, cvt_ref, o_ref, *, rows, half):
    past = ckt_ref.shape[1]
    suffix = _suffix_matrix(DECODE_KEYS, keys_on_rows=False)
    suffix_new = _suffix_matrix(LANES, keys_on_rows=False)
    t, s = _stacked_positions(rows, LANES)
    new_mask = (s < rows) & (s < t)
    lane = lax.broadcasted_iota(jnp.int32, (rows, LANES), 1)
    blocks = [slice(b * DECODE_KEYS, (b + 1) * DECODE_KEYS) for b in range(past // DECODE_KEYS)]
    logits, parts, laters, weights = {}, {}, {}, {}

    def logit(g):
        qs = _stack_halves(q_ref[:, _lanes(g)], half)
        logits[g] = (_dot_nt(qs, _pad_rows(kn_ref[:, _lanes(g)], LANES)),
                     _dot(qs, ckt_ref[_lanes(g), :].astype(BF16)))

    def split(g):
        z_n, z_c = logits.pop(g)
        sp_n, log_sig_n = _softplus_parts(z_n)
        sp_n = jnp.where(new_mask, sp_n, 0.0)
        sp_c, log_sig_c = _softplus_parts(z_c)
        carries = [jnp.sum(sp_n, axis=1, keepdims=True)]
        for cols in reversed(blocks[1:]):
            carries.append(carries[-1] + jnp.sum(sp_c[:, cols], axis=1, keepdims=True))
        parts[g] = (_split_bf16(sp_n), _split_bf16(sp_c))
        laters[g] = (log_sig_n, log_sig_c, carries[::-1])

    def later(g):
        (hi_n, lo_n), (hi_c, lo_c) = parts.pop(g)
        laters[g] += (_dot(hi_n, suffix_new) + _dot(lo_n, suffix_new),
                      [_dot(hi_c[:, cols], suffix) + _dot(lo_c[:, cols], suffix) for cols in blocks])

    def weigh(g):
        log_sig_n, log_sig_c, carries, later_n, later_c = laters.pop(g)
        a_n = jnp.where(new_mask, _exp(log_sig_n - later_n), 0.0).astype(BF16)
        weights[g] = (a_n, [_exp(log_sig_c[:, cols] - later_c[b] - carries[b]).astype(BF16)
                            for b, cols in enumerate(blocks)])

    def update(g):
        a_n, a_c = weights.pop(g)
        vt = cvt_ref[_lanes(g), :].astype(BF16)
        acc = _dot(a_n, _pad_rows(vn_ref[:, _lanes(g)], LANES))
        for b, cols in enumerate(blocks):
            acc = acc + _dot_nt(a_c[b], vt[:, cols])
        o_ref[:, _lanes(g)] = jnp.where(lane < half, acc[:rows], acc[rows:]).astype(BF16)

    _staggered([logit, split, later, weigh, update], q_ref.shape[1] // LANES)


def _decode_attention(body, name, extra_inputs, extra_specs, q, k_new, v_new, cache_kt, cache_v, *, rows):
    n, width = q.shape
    new_spec = pl.BlockSpec((rows, width), lambda b: (b, 0))
    cache_spec = lambda c: pl.BlockSpec((None,) + c.shape[1:], lambda b: (b,) + (0,) * (c.ndim - 1))
    return pl.pallas_call(
        body,
        grid=(n // rows,),
        in_specs=extra_specs + [new_spec, new_spec, new_spec, cache_spec(cache_kt), cache_spec(cache_v)],
        out_specs=new_spec,
        out_shape=jax.ShapeDtypeStruct((n, width), BF16),
        compiler_params=_params(1),
        name=name,
    )(*extra_inputs, q, k_new, v_new, cache_kt, cache_v)


def _merge_ffn_body(*refs, d_ff, seq_rows):
    (x_ref, oa_ref, ob_ref, ga_ref, gb_ref, wa_ref, wb_ref, wo_ref, fg_ref, wup_ref, cw_ref, cb_ref,
     wdn_ref, ng_ref) = refs[:14]
    if seq_rows is None:
        y_ref, conv_ref, acc_ref, carry_ref = refs[14:]
    else:
        h0_ref, h1_ref, y_ref, last2_ref, last1_ref, acc_ref, u_ref = refs[14:]
    d_model = x_ref.shape[1]
    parts = acc_ref.shape[0]
    tm = x_ref.shape[0] // parts
    block = lambda p: slice(p * tm, (p + 1) * tm)
    assert cw_ref.shape[0] == 3

    if seq_rows is None:
        n_slabs = tm // 8
        interleave = lambda a: jnp.swapaxes(a.reshape(8, n_slabs, d_model), 0, 1).reshape(tm, d_model)
        position_order = lambda a: jnp.swapaxes(a.reshape(n_slabs, 8, d_model), 0, 1).reshape(tm, d_model)

        @pl.when(pl.program_id(1) == 0)
        def _():
            carry_ref[...] = jnp.zeros_like(carry_ref)
        sublane = lax.broadcasted_iota(jnp.int32, (8, FFN_COLS), 0)
    else:
        interleave = position_order = lambda a: a
        pos = _mod_pow2(lax.broadcasted_iota(jnp.int32, (tm, FFN_COLS), 0), seq_rows)

    branches, mixes, projected, x1, xn = {}, {}, {}, {}, {}

    def branch(p):
        branches[p] = _dot(oa_ref[block(p), :], wa_ref[...]), _dot(ob_ref[block(p), :], wb_ref[...])

    def mixed(p):
        bra, brb = branches.pop(p)
        mixes[p] = (ga_ref[block(p), :].astype(F32) * bra + gb_ref[block(p), :].astype(F32) * brb).astype(BF16)

    def project(p):
        projected[p] = _dot(mixes.pop(p), wo_ref[...])

    def residual(p):
        x1[p] = interleave(x_ref[block(p), :] + projected.pop(p))
        xn[p] = _rms(x1[p], fg_ref[...]).astype(BF16)

    def finish(p):
        y_ref[block(p), :] = position_order(_rms(x1[p] + acc_ref[p], ng_ref[...]))

    def taps(u, u1, u2, cols):
        return cb_ref[:, cols] + cw_ref[0:1, cols] * u2 + cw_ref[1:2, cols] * u1 + cw_ref[2:3, cols] * u

    def conv(u, cols):
        if seq_rows is None:
            prev2 = carry_ref[7:8, cols]
            prev1 = carry_ref[15:16, cols]
            carry_ref[:, cols] = u[tm - 16:, :]
            conv_ref[:, cols] = u[tm - 16:, :]
            before1 = jnp.where(sublane == 0, prev1, pltpu.roll(u[tm - 8:, :], 1, 0))
            before2 = jnp.where(sublane == 0, prev2, pltpu.roll(u[tm - 16:tm - 8, :], 1, 0))
            u1 = jnp.concatenate([before1, u[:tm - 8, :]], axis=0)
            u2 = jnp.concatenate([before2, before1, u[:tm - 16, :]], axis=0)
            return taps(u, u1, u2, cols)
        else:
            n_seq = tm // seq_rows
            expand = lambda h: jnp.broadcast_to(h[:, None, :], (n_seq, seq_rows, FFN_COLS)).reshape(tm, FFN_COLS)
            prev2 = expand(h0_ref[:, cols])
            prev1 = expand(h1_ref[:, cols])
            for part in range(FFN_COLS // LANES):
                lanes = _lanes(part)
                out = slice(cols.start + part * LANES, cols.start + (part + 1) * LANES)
                u_ref[...] = u[:, lanes]
                last2_ref[:, out] = u_ref[pl.ds(seq_rows - 2, n_seq, stride=seq_rows), :]
                last1_ref[:, out] = u_ref[pl.ds(seq_rows - 1, n_seq, stride=seq_rows), :]
            u1 = jnp.where(pos == 0, prev1, pltpu.roll(u, 1, 0))
            u2 = jnp.where(pos == 0, prev2, jnp.where(pos == 1, prev1, pltpu.roll(u, 2, 0)))
            return taps(u, u1, u2, cols)

    chunks = d_ff // FFN_COLS
    gate_cols = lambda c: slice(c * FFN_COLS, (c + 1) * FFN_COLS)
    val_cols = lambda c: slice(d_ff + c * FFN_COLS, d_ff + (c + 1) * FFN_COLS)
    ups, hidden = {}, {}

    def up(i):
        p, c = divmod(i, chunks)
        ups[i] = _dot(xn[p], wup_ref[:, gate_cols(c)]), _dot(xn[p], wup_ref[:, val_cols(c)])

    def gated(i):
        c = i % chunks
        ug, uv = ups.pop(i)
        gate = conv(ug, gate_cols(c))
        hidden[i] = (gate * conv(uv, val_cols(c)) / (1.0 + _exp(gate, -1.0))).astype(BF16)

    def down(i):
        p, c = divmod(i, chunks)
        part = _dot(hidden.pop(i), wdn_ref[gate_cols(c), :])
        acc_ref[p] = part if c == 0 else acc_ref[p] + part

    merge = [branch, mixed, project, residual]
    for step in merge:
        step(0)
    for t in range(parts * chunks + 2):
        for stage, fn in ((0, up), (2, down), (1, gated)):
            if 0 <= t - stage < parts * chunks:
                fn(t - stage)
        p, c = divmod(t, chunks)
        if p + 1 < parts and 1 <= c <= len(merge):
            merge[c - 1](p + 1)
        if t >= 2 and (t - 2) % chunks == chunks - 1:
            finish((t - 2) // chunks)


def _merge_ffn(x2d, oa, ob, ga, gb, wa, wb, wo, fg, wup, cw, cb, wdn, ng, *, batch, history=None):
    n, d_model = x2d.shape
    width = oa.shape[1]
    d_ff = wdn.shape[0]
    two_ff = wup.shape[1]
    seq = n // batch
    consts = [wa, wb, wo, fg, wup, cw, cb, wdn, ng]
    const_specs = [_const_spec(a.shape) for a in consts]
    if history is None:
        tm = min(FFN_ROWS, seq)
        nt = seq // tm
        grid = (batch, nt)
        row = lambda b, j: (b * nt + j, 0)
        extra, extra_specs = [], []
        conv_shape = [jax.ShapeDtypeStruct((batch, 16, two_ff), F32)]
        conv_spec = [pl.BlockSpec((None, 16, two_ff), lambda b, j: (b, 0, 0))]
        scratch = [pltpu.VMEM((FFN_BLOCKS, tm // FFN_BLOCKS, d_model), F32), pltpu.VMEM((16, two_ff), F32)]
        seq_rows = None
    else:
        tm = n
        grid = (1,)
        row = lambda i: (0, 0)
        extra = list(history)
        extra_specs = [_const_spec(h.shape) for h in history]
        conv_shape = [jax.ShapeDtypeStruct((batch, two_ff), F32)] * 2
        conv_spec = [pl.BlockSpec((batch, two_ff), row)] * 2
        scratch = [pltpu.VMEM((1, tm, d_model), F32), pltpu.VMEM((tm, LANES), F32)]
        seq_rows = seq
    tile = lambda cols: pl.BlockSpec((tm, cols), row)
    return pl.pallas_call(
        functools.partial(_merge_ffn_body, d_ff=d_ff, seq_rows=seq_rows),
        grid=grid,
        in_specs=[tile(d_model), tile(width), tile(width), tile(d_model), tile(d_model)]
                 + const_specs + extra_specs,
        out_specs=[tile(d_model)] + conv_spec,
        out_shape=[jax.ShapeDtypeStruct((n, d_model), F32)] + conv_shape,
        scratch_shapes=scratch,
        compiler_params=_params(len(grid)),
        name="merge_ffn" if history is None else "merge_ffn_streams",
    )(x2d, oa, ob, ga, gb, *consts, *extra)


def kernel(x_prompt, x_sample, cache_diff_k, cache_diff_v, cache_sb_k, cache_sb_v, state_conv,
           attn_norm_g, w_in, lambda_q1, lambda_k1, lambda_q2, lambda_k2, subln_g, w_branch_a,
           w_branch_b, w_out, ffn_norm_g, w_up, conv_w, conv_b, w_down, final_norm_g):
    depth = w_in.shape[0]
    assert depth == 1, "single-layer trunk only"
    batch, seq, d_model = x_prompt.shape
    dec_batch, dec_seq, _ = x_sample.shape
    past = cache_diff_k.shape[2]
    width = d_model // 2
    da_heads, _, da_dim = cache_diff_k.shape[3:]
    da_v = cache_diff_v.shape[-1]
    sb_heads, sb_dim = cache_sb_k.shape[3:]
    assert da_dim == sb_dim and 2 * da_dim == LANES and da_v == LANES
    half = da_dim

    bf = lambda w: w[0].astype(BF16)
    row1 = lambda v: v.reshape(1, -1)
    w_in_bf = bf(w_in)
    consts = (bf(w_branch_a), bf(w_branch_b), bf(w_out), ffn_norm_g, bf(w_up), conv_w[0], conv_b,
              bf(w_down), row1(final_norm_g))
    lams = [row1(v[0]) for v in (lambda_q1, lambda_k1, lambda_q2, lambda_k2)]
    g_row, g_col = row1(subln_g[0]), subln_g[0].reshape(-1, 1)
    specs = lambda arrays: [_const_spec(a.shape) for a in arrays]

    xp = x_prompt.reshape(batch * seq, d_model)
    cos_p, sin_p = _rope_tables(jnp.arange(seq), da_dim)
    (p_dk, p_dv, p_sk, p_sv, qd, kd, vd, qs, ks, vs, ga, gb) = _in_proj(
        xp, attn_norm_g, w_in_bf, cos_p, sin_p, head_dim=da_dim, seq=seq)
    major = lambda c, *dims: jnp.moveaxis(c.reshape(batch, *dims, seq), -1, 1)
    p_dk, p_sk, p_sv = (major(p_dk, da_heads, 2, da_dim), major(p_sk, sb_heads, sb_dim),
                        major(p_sv, sb_heads, sb_dim))
    oa = _prompt_attention(_diff_prompt_body, "diff_prompt", lams + [g_col], specs(lams + [g_col]), qd, kd, vd,
                           batch=batch, seq=seq, rows=DIFF_ROWS, half=half)
    ob = _prompt_attention(_stick_prompt_body, "stick_prompt", [], [], qs, ks, vs,
                           batch=batch, seq=seq, rows=STICK_ROWS, half=half)
    y_p, conv_p = _merge_ffn(xp, oa, ob, ga, gb, *consts, batch=batch)

    xs = x_sample.reshape(dec_batch * dec_seq, d_model)
    cos_s, sin_s = _rope_tables(past + jnp.arange(dec_seq), da_dim)
    cos_s, sin_s = jnp.tile(cos_s, (dec_batch, 1)), jnp.tile(sin_s, (dec_batch, 1))
    (s_dk, s_dv, s_sk, s_sv, qd, kd, vd, qs, ks, vs, ga, gb) = _in_proj(
        xs, attn_norm_g, w_in_bf, cos_s, sin_s, head_dim=da_dim)
    minor = lambda c: jnp.moveaxis(c[0], 1, -1).reshape(dec_batch, width, past)
    oa = _decode_attention(functools.partial(_diff_decode_body, rows=dec_seq, half=half), "diff_decode",
                           lams + [g_row], specs(lams + [g_row]), qd, kd, vd,
                           minor(cache_diff_k), cache_diff_v[0].reshape(dec_batch, past * da_heads, da_v),
                           rows=dec_seq)
    ob = _decode_attention(functools.partial(_stick_decode_body, rows=dec_seq, half=half), "stick_decode",
                           [], [], qs, ks, vs, minor(cache_sb_k), minor(cache_sb_v), rows=dec_seq)
    y_s, last2, last1 = _merge_ffn(xs, oa, ob, ga, gb, *consts, batch=dec_batch,
                                   history=(state_conv[0, :, 0], state_conv[0, :, 1]))

    conv_p = jnp.stack([conv_p[:, 7], conv_p[:, 15]], axis=1)
    conv_s = jnp.stack([last2, last1], axis=1)
    return (y_p.reshape(batch, seq, d_model), y_s.reshape(dec_batch, dec_seq, d_model),
            p_dk[None], p_dv.reshape(depth, batch, seq, da_heads, da_v), p_sk[None], p_sv[None],
            conv_p[None],
            s_dk.reshape(depth, dec_batch, dec_seq, da_heads, 2, da_dim),
            s_dv.reshape(depth, dec_batch, dec_seq, da_heads, da_v),
            s_sk.reshape(depth, dec_batch, dec_seq, sb_heads, sb_dim),
            s_sv.reshape(depth, dec_batch, dec_seq, sb_heads, sb_dim),
            conv_s[None])
```

```python
import functools

import jax
import jax.numpy as jnp
from jax import lax
from jax.experimental import pallas as pl
from jax.experimental.pallas import tpu as pltpu

F32 = jnp.float32
BF16 = jnp.bfloat16

CHUNK = 64
ROPE_THETA = 10000.0
NORM_EPS = 1e-6
NEG_INF = -1e30
LAMBDA_INIT_LAYER0 = 0.8 - 0.6 * 1.0
LOG2_E = 1.4426950408889634
DEAD_SOFTPLUS_SUM = 104.0

LANES = 128
VMEM_LIMIT_BYTES = 56 * 1024 * 1024

PROJ_ROWS = 512
PROJ_BLOCKS = 2
DIFF_ROWS = 512
STICK_ROWS = 256
ATTN_SEQUENCES = 2
FFN_ROWS = 512
FFN_COLS = 256
FFN_BLOCKS = 2
DECODE_KEYS = 256


def _rms(x, g):
    return x * lax.rsqrt(jnp.mean(x * x, axis=-1, keepdims=True) + NORM_EPS) * g


def _dot(a, b):
    return jnp.dot(a, b, preferred_element_type=F32)


def _dot_nt(a, b):
    return lax.dot_general(a, b, (((1,), (1,)), ((), ())), preferred_element_type=F32)


def _log2(n):
    assert n > 0 and n & (n - 1) == 0, n
    return n.bit_length() - 1


def _mod_pow2(x, n):
    _log2(n)
    return jnp.bitwise_and(x, n - 1)


def _div_pow2(x, n):
    return jnp.right_shift(x, _log2(n))


def _lanes(g):
    return slice(g * LANES, (g + 1) * LANES)


def _const_spec(shape):
    return pl.BlockSpec(shape, lambda *_: (0,) * len(shape), pipeline_mode=pl.Buffered(1))


def _params(n_axes, **flags):
    return pltpu.CompilerParams(dimension_semantics=("arbitrary",) * n_axes,
                                vmem_limit_bytes=VMEM_LIMIT_BYTES, flags=flags or None)


def _in_proj_body(x_ref, g_ref, w_ref, cos_ref, sin_ref,
                  kd_ref, vd_ref, ks_ref, vs_ref,
                  qd_bf, kd_bf, vd_bf, qs_bf, ks_bf, vs_bf, ga_ref, gb_ref,
                  *, width, d_model, head_dim, q_scale, position_minor):
    block_rows = x_ref.shape[0] // PROJ_BLOCKS
    xb = {}

    def norm(p):
        rows = slice(p * block_rows, (p + 1) * block_rows)
        xb[p] = _rms(x_ref[rows, :], g_ref[...]).astype(BF16)

    def put(ref, c, z, rows):
        if position_minor:
            ref[_lanes(c), rows] = z.T
        else:
            ref[rows, _lanes(c)] = z

    lane = lax.broadcasted_iota(jnp.int32, (block_rows, LANES), 1)
    half = head_dim // 2
    first_half = _mod_pow2(lane, head_dim) < half

    def rope(z, rows):
        rot = jnp.where(first_half, pltpu.roll(z, LANES - half, 1), pltpu.roll(z, half, 1))
        return z * cos_ref[rows, :] + rot * sin_ref[rows, :]

    tiles = range(width // LANES)

    def diff_q(z, rows):
        for c in tiles:
            qd_bf[rows, _lanes(c)] = (rope(z[:, _lanes(c)], rows) * q_scale).astype(BF16)

    def diff_k(z, rows):
        for c in tiles:
            rk = rope(z[:, _lanes(c)], rows)
            put(kd_ref, c, rk, rows)
            kd_bf[rows, _lanes(c)] = rk.astype(BF16)

    def diff_v(z, rows):
        vd_bf[rows, :] = z.astype(BF16)
        if position_minor:
            for h in tiles:
                vd_ref[pl.ds(rows.start * len(tiles) + h, block_rows, stride=len(tiles)), :] = z[:, _lanes(h)]
        else:
            vd_ref[rows, :] = z

    def stick_q(z, rows):
        qs_bf[rows, :] = (z * q_scale).astype(BF16)

    def stick_kv(out_ref, bf_ref):
        def post(z, rows):
            bf_ref[rows, :] = z.astype(BF16)
            for c in tiles:
                put(out_ref, c, z[:, _lanes(c)], rows)
        return post

    def gate(out_ref, c):
        def post(z, rows):
            out_ref[rows, c * width:(c + 1) * width] = (1.0 / (1.0 + _exp(z, -1.0))).astype(BF16)
        return post

    posts = [diff_q, diff_k, diff_v, stick_q, stick_kv(ks_ref, ks_bf), stick_kv(vs_ref, vs_bf)]
    posts += [gate(ref, c) for ref in (ga_ref, gb_ref) for c in range(d_model // width)]
    units = len(posts)
    z = {}

    def matmul(i):
        p, u = divmod(i, units)
        z[i] = _dot(xb[p], w_ref[:, u * width:(u + 1) * width])

    def post(i):
        p, u = divmod(i, units)
        posts[u](z.pop(i), slice(p * block_rows, (p + 1) * block_rows))

    norm(0)
    for t in range(PROJ_BLOCKS * units + 1):
        if t < PROJ_BLOCKS * units:
            matmul(t)
        if t >= 1:
            post(t - 1)
        if t % units == 1 and t // units + 1 < PROJ_BLOCKS:
            norm(t // units + 1)


def _in_proj(x2d, g, w_bf, cos_t, sin_t, *, head_dim, seq=None):
    n, d_model = x2d.shape
    n_in = w_bf.shape[1]
    width = d_model // 2
    tm = min(PROJ_ROWS, n)
    n_tab = cos_t.shape[0] // tm
    row = lambda i: (i, 0)
    tab = lambda i: (i % n_tab, 0)
    row_spec = pl.BlockSpec((tm, width), row)
    bf_out = jax.ShapeDtypeStruct((n, width), BF16)
    gate_out = jax.ShapeDtypeStruct((n, d_model), BF16)
    if seq is None:
        cache_specs = [row_spec] * 4
        cache_shapes = [jax.ShapeDtypeStruct((n, width), F32)] * 4
    else:
        nt = seq // tm
        minor_spec = pl.BlockSpec((None, width, tm), lambda i: (i // nt, 0, i % nt))
        minor_shape = jax.ShapeDtypeStruct((n // seq, width, seq), F32)
        heads = width // LANES
        cache_specs = [minor_spec, pl.BlockSpec((tm * heads, LANES), row), minor_spec, minor_spec]
        cache_shapes = [minor_shape, jax.ShapeDtypeStruct((n * heads, LANES), F32), minor_shape, minor_shape]
    body = functools.partial(_in_proj_body, width=width, d_model=d_model, head_dim=head_dim,
                             q_scale=head_dim ** -0.5, position_minor=seq is not None)
    return pl.pallas_call(
        body,
        grid=(n // tm,),
        in_specs=[pl.BlockSpec((tm, d_model), row),
                  _const_spec((1, d_model)),
                  _const_spec((d_model, n_in)),
                  pl.BlockSpec((tm, LANES), tab),
                  pl.BlockSpec((tm, LANES), tab)],
        out_specs=cache_specs + [row_spec] * 6 + [pl.BlockSpec((tm, d_model), row)] * 2,
        out_shape=cache_shapes + [bf_out] * 6 + [gate_out] * 2,
        compiler_params=_params(1),
        name="in_proj",
    )(x2d, g, w_bf, cos_t, sin_t)


def _rope_tables(pos, head_dim):
    half = head_dim // 2
    inv = ROPE_THETA ** (-jnp.arange(half, dtype=F32) * 2.0 / head_dim)
    ang = pos.astype(F32)[:, None] * inv[None, :]
    cos = jnp.cos(ang)
    sin = jnp.sin(ang)
    reps = LANES // head_dim
    return (jnp.tile(jnp.concatenate([cos, cos], -1), (1, reps)),
            jnp.tile(jnp.concatenate([-sin, sin], -1), (1, reps)))


def _stack_halves(q, half):
    lane = lax.broadcasted_iota(jnp.int32, q.shape, 1)
    zero = jnp.zeros_like(q)
    return jnp.concatenate([jnp.where(lane < half, q, zero), jnp.where(lane >= half, q, zero)], axis=0)


def _exp(x, sign=1.0):
    return jnp.exp2(x * (sign * LOG2_E))


def _softplus_parts(z):
    sp = jnp.maximum(z, 0.0) + jnp.log(1.0 + _exp(jnp.abs(z), -1.0))
    return sp, z - sp


def _split_bf16(x):
    hi = x.astype(BF16)
    return hi, (x - hi.astype(F32)).astype(BF16)


def _suffix_matrix(n, keys_on_rows):
    a = lax.broadcasted_iota(jnp.int32, (n, n), 0)
    b = lax.broadcasted_iota(jnp.int32, (n, n), 1)
    after = (b > a) if keys_on_rows else (a > b)
    return jnp.where(after, 1.0, 0.0).astype(BF16)


def _lambda_value(lq1, lk1, lq2, lk2):
    return (jnp.exp(jnp.sum(lq1[...] * lk1[...], axis=-1, keepdims=True))
            - jnp.exp(jnp.sum(lq2[...] * lk2[...], axis=-1, keepdims=True))
            + LAMBDA_INIT_LAYER0)


def _staggered(stages, n, order=None):
    order = order or [0] + list(range(len(stages) - 1, 0, -1))
    for t in range(n + len(stages) - 1):
        for s in order:
            if 0 <= t - s < n:
                stages[s](t - s)


def _softmax_weights_t(s, m, l, mask):
    if mask is not None:
        s = jnp.where(mask, s, NEG_INF)
    m_new = jnp.maximum(m, jnp.max(s, axis=0, keepdims=True))
    alpha = _exp(m - m_new)
    p = _exp(s - m_new)
    return m_new, alpha * l + jnp.sum(p, axis=0, keepdims=True), alpha, p.astype(BF16)


def _stick_parts_t(z, mask):
    sp, log_sig = _softplus_parts(z)
    if mask is not None:
        sp = jnp.where(mask, sp, 0.0)
    return log_sig, jnp.sum(sp, axis=0, keepdims=True), _split_bf16(sp)


def _stick_weights_t(log_sig, later, r, mask):
    a = _exp(log_sig - later - r)
    if mask is not None:
        a = jnp.where(mask, a, 0.0)
    return a.astype(BF16)


def _key_query_positions(n_keys, rows):
    s = lax.broadcasted_iota(jnp.int32, (n_keys, 2 * rows), 0)
    t = _mod_pow2(lax.broadcasted_iota(jnp.int32, (n_keys, 2 * rows), 1), rows)
    return s, t


def _diff_finish_t(acc, l, lam, g_col, rows):
    o = acc / l
    o = o[:, :rows] - lam * o[:, rows:]
    y = o * lax.rsqrt(jnp.mean(o * o, axis=0, keepdims=True) + NORM_EPS) * g_col
    return (y * (1.0 - LAMBDA_INIT_LAYER0)).T.astype(BF16)


def _stick_finish_t(acc, rows, half):
    row = lax.broadcasted_iota(jnp.int32, (LANES, rows), 0)
    return jnp.where(row < half, acc[:, :rows], acc[:, rows:]).T.astype(BF16)


def _attention_units(q_ref):
    return [(seq, g) for seq in range(q_ref.shape[0]) for g in range(q_ref.shape[2] // LANES)]


def _transpose_values(v_ref, vt_ref):
    nblk, _, n = vt_ref.shape
    for c in range(nblk):
        vt_ref[c] = v_ref[c * n:(c + 1) * n, :].astype(F32).T.astype(BF16)


def _diff_prompt_body(lq1, lk1, lq2, lk2, g_ref, q_ref, k_ref, v_ref, o_ref, vt_ref, acc_ref, *, rows, half):
    i = pl.program_id(1)
    units = _attention_units(q_ref)

    @pl.when(i == 0)
    def _():
        for seq in range(q_ref.shape[0]):
            _transpose_values(v_ref.at[seq], vt_ref.at[seq])

    qs = [_stack_halves(q_ref[seq, :, _lanes(g)], half) for seq, g in units]
    acc_ref[...] = jnp.zeros_like(acc_ref)

    def step(j, stats, mask):
        off = pl.multiple_of(j * rows, rows)
        scores, weights, out = {}, {}, {}

        def score(u):
            seq, g = units[u]
            scores[u] = _dot_nt(k_ref[seq, pl.ds(off, rows), _lanes(g)], qs[u])

        def weigh(u):
            m, l, alpha, p = _softmax_weights_t(scores.pop(u), *stats[u], mask)
            out[u] = (m, l)
            weights[u] = (alpha, p)

        def update(u):
            seq, g = units[u]
            alpha, p = weights.pop(u)
            acc_ref[u] = alpha * acc_ref[u] + _dot(vt_ref[seq, j, _lanes(g), :], p)

        _staggered([score, weigh, update], len(units))
        return tuple(out[u] for u in range(len(units)))

    init = tuple((jnp.full((1, 2 * rows), NEG_INF, F32), jnp.zeros((1, 2 * rows), F32)) for _ in units)
    stats = lax.fori_loop(0, i, lambda j, st: step(j, st, None), init)
    s, t = _key_query_positions(rows, rows)
    stats = step(i, stats, _div_pow2(s, CHUNK) <= _div_pow2(t, CHUNK))
    lam = _lambda_value(lq1, lk1, lq2, lk2)
    for u, (seq, g) in enumerate(units):
        o_ref[seq, :, _lanes(g)] = _diff_finish_t(acc_ref[u], stats[u][1], lam, g_ref[...], rows)


def _stick_prompt_body(q_ref, k_ref, v_ref, o_ref, vt_ref, acc_ref, *, rows, half):
    i = pl.program_id(1)
    units = _attention_units(q_ref)

    @pl.when(i == 0)
    def _():
        for seq in range(q_ref.shape[0]):
            _transpose_values(v_ref.at[seq], vt_ref.at[seq])

    qs = [_stack_halves(q_ref[seq, :, _lanes(g)], half) for seq, g in units]
    suffix = _suffix_matrix(rows, keys_on_rows=True)
    acc_ref[...] = jnp.zeros_like(acc_ref)

    def step(j, carry, mask):
        off = pl.multiple_of(j * rows, rows)
        logits, parts, sums, weights, out = {}, {}, {}, {}, {}

        def logit(u):
            seq, g = units[u]
            logits[u] = _dot_nt(k_ref[seq, pl.ds(off, rows), _lanes(g)], qs[u])

        def split(u):
            log_sig, total, parts[u] = _stick_parts_t(logits.pop(u), mask)
            out[u] = carry[u] + total
            sums[u] = log_sig

        def later(u):
            hi, lo = parts.pop(u)
            sums[u] = (sums[u], _dot(suffix, hi) + _dot(suffix, lo))

        def weigh(u):
            weights[u] = _stick_weights_t(*sums.pop(u), carry[u], mask)

        def update(u):
            seq, g = units[u]
            acc_ref[u] += _dot(vt_ref[seq, j, _lanes(g), :], weights.pop(u))

        _staggered([logit, split, later, weigh, update], len(units))
        return tuple(out[u] for u in range(len(units)))

    def live(carry):
        return jnp.min(functools.reduce(jnp.minimum, carry)) < DEAD_SOFTPLUS_SUM

    def older_block(state):
        jj, _, carry = state
        carry = step(i - 1 - jj, carry, None)
        return jj + 1, live(carry), carry

    s, t = _key_query_positions(rows, rows)
    carry = step(i, tuple(jnp.zeros((1, 2 * rows), F32) for _ in units), s < t)
    lax.while_loop(lambda state: (state[0] < i) & state[1], older_block, (jnp.int32(0), live(carry), carry))
    for u, (seq, g) in enumerate(units):
        o_ref[seq, :, _lanes(g)] = _stick_finish_t(acc_ref[u], rows, half)


def _prompt_attention(body, name, extra_inputs, extra_specs, q, k, v, *, batch, seq, rows, half):
    n, width = q.shape
    body = functools.partial(body, rows=rows, half=half)
    nq = seq // rows
    together = ATTN_SEQUENCES * STICK_ROWS // rows
    by_seq = lambda a: a.reshape(batch, seq, width)
    q_spec = pl.BlockSpec((together, rows, width), lambda b, i: (b, i, 0))
    kv_spec = pl.BlockSpec((together, seq, width), lambda b, i: (b, 0, 0))
    return pl.pallas_call(
        body,
        grid=(batch // together, nq),
        in_specs=extra_specs + [q_spec, kv_spec, kv_spec],
        out_specs=q_spec,
        out_shape=jax.ShapeDtypeStruct((batch, seq, width), BF16),
        scratch_shapes=[pltpu.VMEM((together, nq, width, rows), BF16),
                        pltpu.VMEM((together * (width // LANES), LANES, 2 * rows), F32)],
        compiler_params=_params(2),
        name=name,
    )(*extra_inputs, by_seq(q), by_seq(k), by_seq(v)).reshape(n, width)


def _pad_rows(x, n):
    return jnp.concatenate([x, jnp.zeros((n - x.shape[0], x.shape[1]), x.dtype)], axis=0)


def _stacked_positions(rows, n_keys):
    t = _mod_pow2(lax.broadcasted_iota(jnp.int32, (2 * rows, n_keys), 0), rows)
    s = lax.broadcasted_iota(jnp.int32, (2 * rows, n_keys), 1)
    return t, s


def _diff_decode_body(lq1, lk1, lq2, lk2, g_ref, q_ref, kn_ref, vn_ref, ckt_ref, cv_ref, o_ref,
                      *, rows, half):
    past = ckt_ref.shape[1]
    lam = _lambda_value(lq1, lk1, lq2, lk2)
    t, s = _stacked_positions(rows, LANES)
    new_mask = (s < rows) & (_div_pow2(past + s, CHUNK) <= _div_pow2(past + t, CHUNK))
    groups = q_ref.shape[1] // LANES
    scores, weights = {}, {}

    def score(g):
        qs = _stack_halves(q_ref[:, _lanes(g)], half)
        scores[g] = (_dot(qs, ckt_ref[_lanes(g), :].astype(BF16)),
                     _dot_nt(qs, _pad_rows(kn_ref[:, _lanes(g)], LANES)))

    def weigh(g):
        s_c, s_n = scores.pop(g)
        s_n = jnp.where(new_mask, s_n, NEG_INF)
        m = jnp.maximum(jnp.max(s_c, axis=1, keepdims=True), jnp.max(s_n, axis=1, keepdims=True))
        p_c = _exp(s_c - m)
        p_n = _exp(s_n - m)
        l = jnp.sum(p_c, axis=1, keepdims=True) + jnp.sum(p_n, axis=1, keepdims=True)
        weights[g] = (p_c.astype(BF16), p_n.astype(BF16), l)

    def update(g):
        p_c, p_n, l = weights.pop(g)
        v_c = cv_ref[pl.ds(g, past, stride=groups), :]
        o = (_dot(p_c, v_c.astype(BF16)) + _dot(p_n, _pad_rows(vn_ref[:, _lanes(g)], LANES))) / l
        o = o[:rows] - lam * o[rows:]
        o_ref[:, _lanes(g)] = (_rms(o, g_ref[...]) * (1.0 - LAMBDA_INIT_LAYER0)).astype(BF16)

    _staggered([score, weigh, update], groups)


def _stick_decode_body(q_ref, kn_ref, vn_ref, ckt_ref, cvt_ref, o_ref, *, rows, half):
    past = ckt_ref.shape[1]
    suffix = _suffix_matrix(DECODE_KEYS, keys_on_rows=False)
    suffix_new = _suffix_matrix(LANES, keys_on_rows=False)
    t, s = _stacked_positions(rows, LANES)
    new_mask = (s < rows) & (s < t)
    lane = lax.broadcasted_iota(jnp.int32, (rows, LANES), 1)
    blocks = [slice(b * DECODE_KEYS, (b + 1) * DECODE_KEYS) for b in range(past // DECODE_KEYS)]
    logits, parts, laters, weights = {}, {}, {}, {}

    def logit(g):
        qs = _stack_halves(q_ref[:, _lanes(g)], half)
        logits[g] = (_dot_nt(qs, _pad_rows(kn_ref[:, _lanes(g)], LANES)),
                     _dot(qs, ckt_ref[_lanes(g), :].astype(BF16)))

    def split(g):
        z_n, z_c = logits.pop(g)
        sp_n, log_sig_n = _softplus_parts(z_n)
        sp_n = jnp.where(new_mask, sp_n, 0.0)
        sp_c, log_sig_c = _softplus_parts(z_c)
        carries = [jnp.sum(sp_n, axis=1, keepdims=True)]
        for cols in reversed(blocks[1:]):
            carries.append(carries[-1] + jnp.sum(sp_c[:, cols], axis=1, keepdims=True))
        parts[g] = (_split_bf16(sp_n), _split_bf16(sp_c))
        laters[g] = (log_sig_n, log_sig_c, carries[::-1])

    def later(g):
        (hi_n, lo_n), (hi_c, lo_c) = parts.pop(g)
        laters[g] += (_dot(hi_n, suffix_new) + _dot(lo_n, suffix_new),
                      [_dot(hi_c[:, cols], suffix) + _dot(lo_c[:, cols], suffix) for cols in blocks])

    def weigh(g):
        log_sig_n, log_sig_c, carries, later_n, later_c = laters.pop(g)
        a_n = jnp.where(new_mask, _exp(log_sig_n - later_n), 0.0).astype(BF16)
        weights[g] = (a_n, [_exp(log_sig_c[:, cols] - later_c[b] - carries[b]).astype(BF16)
                            for b, cols in enumerate(blocks)])

    def update(g):
        a_n, a_c = weights.pop(g)
        vt = cvt_ref[_lanes(g), :].astype(BF16)
        acc = _dot(a_n, _pad_rows(vn_ref[:, _lanes(g)], LANES))
        for b, cols in enumerate(blocks):
            acc = acc + _dot_nt(a_c[b], vt[:, cols])
        o_ref[:, _lanes(g)] = jnp.where(lane < half, acc[:rows], acc[rows:]).astype(BF16)

    _staggered([logit, split, later, weigh, update], q_ref.shape[1] // LANES)


def _decode_attention(body, name, extra_inputs, extra_specs, q, k_new, v_new, cache_kt, cache_v, *, rows):
    n, width = q.shape
    new_spec = pl.BlockSpec((rows, width), lambda b: (b, 0))
    cache_spec = lambda c: pl.BlockSpec((None,) + c.shape[1:], lambda b: (b,) + (0,) * (c.ndim - 1))
    return pl.pallas_call(
        body,
        grid=(n // rows,),
        in_specs=extra_specs + [new_spec, new_spec, new_spec, cache_spec(cache_kt), cache_spec(cache_v)],
        out_specs=new_spec,
        out_shape=jax.ShapeDtypeStruct((n, width), BF16),
        compiler_params=_params(1),
        name=name,
    )(*extra_inputs, q, k_new, v_new, cache_kt, cache_v)


def _merge_ffn_body(*refs, d_ff, seq_rows):
    (x_ref, oa_ref, ob_ref, ga_ref, gb_ref, wa_ref, wb_ref, wo_ref, fg_ref, wup_ref, cw_ref, cb_ref,
     wdn_ref, ng_ref) = refs[:14]
    if seq_rows is None:
        y_ref, conv_ref, acc_ref, carry_ref = refs[14:]
    else:
        h0_ref, h1_ref, y_ref, last2_ref, last1_ref, acc_ref, u_ref = refs[14:]
    d_model = x_ref.shape[1]
    parts = acc_ref.shape[0]
    tm = x_ref.shape[0] // parts
    block = lambda p: slice(p * tm, (p + 1) * tm)
    assert cw_ref.shape[0] == 3

    if seq_rows is None:
        n_slabs = tm // 8
        interleave = lambda a: jnp.swapaxes(a.reshape(8, n_slabs, d_model), 0, 1).reshape(tm, d_model)
        position_order = lambda a: jnp.swapaxes(a.reshape(n_slabs, 8, d_model), 0, 1).reshape(tm, d_model)

        @pl.when(pl.program_id(1) == 0)
        def _():
            carry_ref[...] = jnp.zeros_like(carry_ref)
        sublane = lax.broadcasted_iota(jnp.int32, (8, FFN_COLS), 0)
    else:
        interleave = position_order = lambda a: a
        pos = _mod_pow2(lax.broadcasted_iota(jnp.int32, (tm, FFN_COLS), 0), seq_rows)

    branches, mixes, projected, x1, xn = {}, {}, {}, {}, {}

    def branch(p):
        branches[p] = _dot(oa_ref[block(p), :], wa_ref[...]), _dot(ob_ref[block(p), :], wb_ref[...])

    def mixed(p):
        bra, brb = branches.pop(p)
        mixes[p] = (ga_ref[block(p), :].astype(F32) * bra + gb_ref[block(p), :].astype(F32) * brb).astype(BF16)

    def project(p):
        projected[p] = _dot(mixes.pop(p), wo_ref[...])

    def residual(p):
        x1[p] = interleave(x_ref[block(p), :] + projected.pop(p))
        xn[p] = _rms(x1[p], fg_ref[...]).astype(BF16)

    def finish(p):
        y_ref[block(p), :] = position_order(_rms(x1[p] + acc_ref[p], ng_ref[...]))

    def taps(u, u1, u2, cols):
        return cb_ref[:, cols] + cw_ref[0:1, cols] * u2 + cw_ref[1:2, cols] * u1 + cw_ref[2:3, cols] * u

    def conv(u, cols):
        if seq_rows is None:
            prev2 = carry_ref[7:8, cols]
            prev1 = carry_ref[15:16, cols]
            carry_ref[:, cols] = u[tm - 16:, :]
            conv_ref[:, cols] = u[tm - 16:, :]
            before1 = jnp.where(sublane == 0, prev1, pltpu.roll(u[tm - 8:, :], 1, 0))
            before2 = jnp.where(sublane == 0, prev2, pltpu.roll(u[tm - 16:tm - 8, :], 1, 0))
            u1 = jnp.concatenate([before1, u[:tm - 8, :]], axis=0)
            u2 = jnp.concatenate([before2, before1, u[:tm - 16, :]], axis=0)
            return taps(u, u1, u2, cols)
        else:
            n_seq = tm // seq_rows
            expand = lambda h: jnp.broadcast_to(h[:, None, :], (n_seq, seq_rows, FFN_COLS)).reshape(tm, FFN_COLS)
            prev2 = expand(h0_ref[:, cols])
            prev1 = expand(h1_ref[:, cols])
            for part in range(FFN_COLS // LANES):
                lanes = _lanes(part)
                out = slice(cols.start + part * LANES, cols.start + (part + 1) * LANES)
                u_ref[...] = u[:, lanes]
                last2_ref[:, out] = u_ref[pl.ds(seq_rows - 2, n_seq, stride=seq_rows), :]
                last1_ref[:, out] = u_ref[pl.ds(seq_rows - 1, n_seq, stride=seq_rows), :]
            u1 = jnp.where(pos == 0, prev1, pltpu.roll(u, 1, 0))
            u2 = jnp.where(pos == 0, prev2, jnp.where(pos == 1, prev1, pltpu.roll(u, 2, 0)))
            return taps(u, u1, u2, cols)

    chunks = d_ff // FFN_COLS
    gate_cols = lambda c: slice(c * FFN_COLS, (c + 1) * FFN_COLS)
    val_cols = lambda c: slice(d_ff + c * FFN_COLS, d_ff + (c + 1) * FFN_COLS)
    ups, hidden = {}, {}

    def up(i):
        p, c = divmod(i, chunks)
        ups[i] = _dot(xn[p], wup_ref[:, gate_cols(c)]), _dot(xn[p], wup_ref[:, val_cols(c)])

    def gated(i):
        c = i % chunks
        ug, uv = ups.pop(i)
        gate = conv(ug, gate_cols(c))
        hidden[i] = (gate * conv(uv, val_cols(c)) / (1.0 + _exp(gate, -1.0))).astype(BF16)

    def down(i):
        p, c = divmod(i, chunks)
        part = _dot(hidden.pop(i), wdn_ref[gate_cols(c), :])
        acc_ref[p] = part if c == 0 else acc_ref[p] + part

    merge = [branch, mixed, project, residual]
    for step in merge:
        step(0)
    for t in range(parts * chunks + 2):
        for stage, fn in ((0, up), (2, down), (1, gated)):
            if 0 <= t - stage < parts * chunks:
                fn(t - stage)
        p, c = divmod(t, chunks)
        if p + 1 < parts and 1 <= c <= len(merge):
            merge[c - 1](p + 1)
        if t >= 2 and (t - 2) % chunks == chunks - 1:
            finish((t - 2) // chunks)


def _merge_ffn(x2d, oa, ob, ga, gb, wa, wb, wo, fg, wup, cw, cb, wdn, ng, *, batch, history=None):
    n, d_model = x2d.shape
    width = oa.shape[1]
    d_ff = wdn.shape[0]
    two_ff = wup.shape[1]
    seq = n // batch
    consts = [wa, wb, wo, fg, wup, cw, cb, wdn, ng]
    const_specs = [_const_spec(a.shape) for a in consts]
    if history is None:
        tm = min(FFN_ROWS, seq)
        nt = seq // tm
        grid = (batch, nt)
        row = lambda b, j: (b * nt + j, 0)
        extra, extra_specs = [], []
        conv_shape = [jax.ShapeDtypeStruct((batch, 16, two_ff), F32)]
        conv_spec = [pl.BlockSpec((None, 16, two_ff), lambda b, j: (b, 0, 0))]
        scratch = [pltpu.VMEM((FFN_BLOCKS, tm // FFN_BLOCKS, d_model), F32), pltpu.VMEM((16, two_ff), F32)]
        seq_rows = None
    else:
        tm = n
        grid = (1,)
        row = lambda i: (0, 0)
        extra = list(history)
        extra_specs = [_const_spec(h.shape) for h in history]
        conv_shape = [jax.ShapeDtypeStruct((batch, two_ff), F32)] * 2
        conv_spec = [pl.BlockSpec((batch, two_ff), row)] * 2
        scratch = [pltpu.VMEM((1, tm, d_model), F32), pltpu.VMEM((tm, LANES), F32)]
        seq_rows = seq
    tile = lambda cols: pl.BlockSpec((tm, cols), row)
    return pl.pallas_call(
        functools.partial(_merge_ffn_body, d_ff=d_ff, seq_rows=seq_rows),
        grid=grid,
        in_specs=[tile(d_model), tile(width), tile(width), tile(d_model), tile(d_model)]
                 + const_specs + extra_specs,
        out_specs=[tile(d_model)] + conv_spec,
        out_shape=[jax.ShapeDtypeStruct((n, d_model), F32)] + conv_shape,
        scratch_shapes=scratch,
        compiler_params=_params(len(grid)),
        name="merge_ffn" if history is None else "merge_ffn_streams",
    )(x2d, oa, ob, ga, gb, *consts, *extra)


def kernel(x_prompt, x_sample, cache_diff_k, cache_diff_v, cache_sb_k, cache_sb_v, state_conv,
           attn_norm_g, w_in, lambda_q1, lambda_k1, lambda_q2, lambda_k2, subln_g, w_branch_a,
           w_branch_b, w_out, ffn_norm_g, w_up, conv_w, conv_b, w_down, final_norm_g):
    depth = w_in.shape[0]
    assert depth == 1, "single-layer trunk only"
    batch, seq, d_model = x_prompt.shape
    dec_batch, dec_seq, _ = x_sample.shape
    past = cache_diff_k.shape[2]
    width = d_model // 2
    da_heads, _, da_dim = cache_diff_k.shape[3:]
    da_v = cache_diff_v.shape[-1]
    sb_heads, sb_dim = cache_sb_k.shape[3:]
    assert da_dim == sb_dim and 2 * da_dim == LANES and da_v == LANES
    half = da_dim

    bf = lambda w: w[0].astype(BF16)
    row1 = lambda v: v.reshape(1, -1)
    w_in_bf = bf(w_in)
    consts = (bf(w_branch_a), bf(w_branch_b), bf(w_out), ffn_norm_g, bf(w_up), conv_w[0], conv_b,
              bf(w_down), row1(final_norm_g))
    lams = [row1(v[0]) for v in (lambda_q1, lambda_k1, lambda_q2, lambda_k2)]
    g_row, g_col = row1(subln_g[0]), subln_g[0].reshape(-1, 1)
    specs = lambda arrays: [_const_spec(a.shape) for a in arrays]

    xp = x_prompt.reshape(batch * seq, d_model)
    cos_p, sin_p = _rope_tables(jnp.arange(seq), da_dim)
    (p_dk, p_dv, p_sk, p_sv, qd, kd, vd, qs, ks, vs, ga, gb) = _in_proj(
        xp, attn_norm_g, w_in_bf, cos_p, sin_p, head_dim=da_dim, seq=seq)
    major = lambda c, *dims: jnp.moveaxis(c.reshape(batch, *dims, seq), -1, 1)
    p_dk, p_sk, p_sv = (major(p_dk, da_heads, 2, da_dim), major(p_sk, sb_heads, sb_dim),
                        major(p_sv, sb_heads, sb_dim))
    oa = _prompt_attention(_diff_prompt_body, "diff_prompt", lams + [g_col], specs(lams + [g_col]), qd, kd, vd,
                           batch=batch, seq=seq, rows=DIFF_ROWS, half=half)
    ob = _prompt_attention(_stick_prompt_body, "stick_prompt", [], [], qs, ks, vs,
                           batch=batch, seq=seq, rows=STICK_ROWS, half=half)
    y_p, conv_p = _merge_ffn(xp, oa, ob, ga, gb, *consts, batch=batch)

    xs = x_sample.reshape(dec_batch * dec_seq, d_model)
    cos_s, sin_s = _rope_tables(past + jnp.arange(dec_seq), da_dim)
    cos_s, sin_s = jnp.tile(cos_s, (dec_batch, 1)), jnp.tile(sin_s, (dec_batch, 1))
    (s_dk, s_dv, s_sk, s_sv, qd, kd, vd, qs, ks, vs, ga, gb) = _in_proj(
        xs, attn_norm_g, w_in_bf, cos_s, sin_s, head_dim=da_dim)
    minor = lambda c: jnp.moveaxis(c[0], 1, -1).reshape(dec_batch, width, past)
    oa = _decode_attention(functools.partial(_diff_decode_body, rows=dec_seq, half=half), "diff_decode",
                           lams + [g_row], specs(lams + [g_row]), qd, kd, vd,
                           minor(cache_diff_k), cache_diff_v[0].reshape(dec_batch, past * da_heads, da_v),
                           rows=dec_seq)
    ob = _decode_attention(functools.partial(_stick_decode_body, rows=dec_seq, half=half), "stick_decode",
                           [], [], qs, ks, vs, minor(cache_sb_k), minor(cache_sb_v), rows=dec_seq)
    y_s, last2, last1 = _merge_ffn(xs, oa, ob, ga, gb, *consts, batch=dec_batch,
                                   history=(state_conv[0, :, 0], state_conv[0, :, 1]))

    conv_p = jnp.stack([conv_p[:, 7], conv_p[:, 15]], axis=1)
    conv_s = jnp.stack([last2, last1], axis=1)
    return (y_p.reshape(batch, seq, d_model), y_s.reshape(dec_batch, dec_seq, d_model),
            p_dk[None], p_dv.reshape(depth, batch, seq, da_heads, da_v), p_sk[None], p_sv[None],
            conv_p[None],
            s_dk.reshape(depth, dec_batch, dec_seq, da_heads, 2, da_dim),
            s_dv.reshape(depth, dec_batch, dec_seq, da_heads, da_v),
            s_sk.reshape(depth, dec_batch, dec_seq, sb_heads, sb_dim),
            s_sv.reshape(depth, dec_batch, dec_seq, sb_heads, sb_dim),
            conv_s[None])
```

```python
import functools

import jax
import jax.numpy as jnp
from jax import lax
from jax.experimental import pallas as pl
from jax.experimental.pallas import tpu as pltpu

F32 = jnp.float32
BF16 = jnp.bfloat16

CHUNK = 64
ROPE_THETA = 10000.0
NORM_EPS = 1e-6
NEG_INF = -1e30
LAMBDA_INIT_LAYER0 = 0.8 - 0.6 * 1.0
LOG2_E = 1.4426950408889634
DEAD_SOFTPLUS_SUM = 104.0

LANES = 128
VMEM_LIMIT_BYTES = 56 * 1024 * 1024

PROJ_ROWS = 512
PROJ_BLOCKS = 2
DIFF_ROWS = 512
STICK_ROWS = 256
ATTN_SEQUENCES = 2
FFN_ROWS = 512
FFN_COLS = 256
FFN_BLOCKS = 2
DECODE_KEYS = 256


def _rms(x, g):
    return x * lax.rsqrt(jnp.mean(x * x, axis=-1, keepdims=True) + NORM_EPS) * g


def _dot(a, b):
    return jnp.dot(a, b, preferred_element_type=F32)


def _dot_nt(a, b):
    return lax.dot_general(a, b, (((1,), (1,)), ((), ())), preferred_element_type=F32)


def _log2(n):
    assert n > 0 and n & (n - 1) == 0, n
    return n.bit_length() - 1


def _mod_pow2(x, n):
    _log2(n)
    return jnp.bitwise_and(x, n - 1)


def _div_pow2(x, n):
    return jnp.right_shift(x, _log2(n))


def _lanes(g):
    return slice(g * LANES, (g + 1) * LANES)


def _const_spec(shape):
    return pl.BlockSpec(shape, lambda *_: (0,) * len(shape), pipeline_mode=pl.Buffered(1))


def _params(n_axes, **flags):
    return pltpu.CompilerParams(dimension_semantics=("arbitrary",) * n_axes,
                                vmem_limit_bytes=VMEM_LIMIT_BYTES, flags=flags or None)


def _in_proj_body(x_ref, g_ref, w_ref, cos_ref, sin_ref,
                  kd_ref, vd_ref, ks_ref, vs_ref,
                  qd_bf, kd_bf, vd_bf, qs_bf, ks_bf, vs_bf, ga_ref, gb_ref,
                  *, width, d_model, head_dim, q_scale, position_minor):
    block_rows = x_ref.shape[0] // PROJ_BLOCKS
    xb = {}

    def norm(p):
        rows = slice(p * block_rows, (p + 1) * block_rows)
        xb[p] = _rms(x_ref[rows, :], g_ref[...]).astype(BF16)

    def put(ref, c, z, rows):
        if position_minor:
            ref[_lanes(c), rows] = z.T
        else:
            ref[rows, _lanes(c)] = z

    lane = lax.broadcasted_iota(jnp.int32, (block_rows, LANES), 1)
    half = head_dim // 2
    first_half = _mod_pow2(lane, head_dim) < half

    def rope(z, rows):
        rot = jnp.where(first_half, pltpu.roll(z, LANES - half, 1), pltpu.roll(z, half, 1))
        return z * cos_ref[rows, :] + rot * sin_ref[rows, :]

    tiles = range(width // LANES)

    def diff_q(z, rows):
        for c in tiles:
            qd_bf[rows, _lanes(c)] = (rope(z[:, _lanes(c)], rows) * q_scale).astype(BF16)

    def diff_k(z, rows):
        for c in tiles:
            rk = rope(z[:, _lanes(c)], rows)
            put(kd_ref, c, rk, rows)
            kd_bf[rows, _lanes(c)] = rk.astype(BF16)

    def diff_v(z, rows):
        vd_bf[rows, :] = z.astype(BF16)
        if position_minor:
            for h in tiles:
                vd_ref[pl.ds(rows.start * len(tiles) + h, block_rows, stride=len(tiles)), :] = z[:, _lanes(h)]
        else:
            vd_ref[rows, :] = z

    def stick_q(z, rows):
        qs_bf[rows, :] = (z * q_scale).astype(BF16)

    def stick_kv(out_ref, bf_ref):
        def post(z, rows):
            bf_ref[rows, :] = z.astype(BF16)
            for c in tiles:
                put(out_ref, c, z[:, _lanes(c)], rows)
        return post

    def gate(out_ref, c):
        def post(z, rows):
            out_ref[rows, c * width:(c + 1) * width] = (1.0 / (1.0 + _exp(z, -1.0))).astype(BF16)
        return post

    posts = [diff_q, diff_k, diff_v, stick_q, stick_kv(ks_ref, ks_bf), stick_kv(vs_ref, vs_bf)]
    posts += [gate(ref, c) for ref in (ga_ref, gb_ref) for c in range(d_model // width)]
    units = len(posts)
    z = {}

    def matmul(i):
        p, u = divmod(i, units)
        z[i] = _dot(xb[p], w_ref[:, u * width:(u + 1) * width])

    def post(i):
        p, u = divmod(i, units)
        posts[u](z.pop(i), slice(p * block_rows, (p + 1) * block_rows))

    norm(0)
    for t in range(PROJ_BLOCKS * units + 1):
        if t < PROJ_BLOCKS * units:
            matmul(t)
        if t >= 1:
            post(t - 1)
        if t % units == 1 and t // units + 1 < PROJ_BLOCKS:
            norm(t // units + 1)


def _in_proj(x2d, g, w_bf, cos_t, sin_t, *, head_dim, seq=None):
    n, d_model = x2d.shape
    n_in = w_bf.shape[1]
    width = d_model // 2
    tm = min(PROJ_ROWS, n)
    n_tab = cos_t.shape[0] // tm
    row = lambda i: (i, 0)
    tab = lambda i: (i % n_tab, 0)
    row_spec = pl.BlockSpec((tm, width), row)
    bf_out = jax.ShapeDtypeStruct((n, width), BF16)
    gate_out = jax.ShapeDtypeStruct((n, d_model), BF16)
    if seq is None:
        cache_specs = [row_spec] * 4
        cache_shapes = [jax.ShapeDtypeStruct((n, width), F32)] * 4
    else:
        nt = seq // tm
        minor_spec = pl.BlockSpec((None, width, tm), lambda i: (i // nt, 0, i % nt))
        minor_shape = jax.ShapeDtypeStruct((n // seq, width, seq), F32)
        heads = width // LANES
        cache_specs = [minor_spec, pl.BlockSpec((tm * heads, LANES), row), minor_spec, minor_spec]
        cache_shapes = [minor_shape, jax.ShapeDtypeStruct((n * heads, LANES), F32), minor_shape, minor_shape]
    body = functools.partial(_in_proj_body, width=width, d_model=d_model, head_dim=head_dim,
                             q_scale=head_dim ** -0.5, position_minor=seq is not None)
    return pl.pallas_call(
        body,
        grid=(n // tm,),
        in_specs=[pl.BlockSpec((tm, d_model), row),
                  _const_spec((1, d_model)),
                  _const_spec((d_model, n_in)),
                  pl.BlockSpec((tm, LANES), tab),
                  pl.BlockSpec((tm, LANES), tab)],
        out_specs=cache_specs + [row_spec] * 6 + [pl.BlockSpec((tm, d_model), row)] * 2,
        out_shape=cache_shapes + [bf_out] * 6 + [gate_out] * 2,
        compiler_params=_params(1),
        name="in_proj",
    )(x2d, g, w_bf, cos_t, sin_t)


def _rope_tables(pos, head_dim):
    half = head_dim // 2
    inv = ROPE_THETA ** (-jnp.arange(half, dtype=F32) * 2.0 / head_dim)
    ang = pos.astype(F32)[:, None] * inv[None, :]
    cos = jnp.cos(ang)
    sin = jnp.sin(ang)
    reps = LANES // head_dim
    return (jnp.tile(jnp.concatenate([cos, cos], -1), (1, reps)),
            jnp.tile(jnp.concatenate([-sin, sin], -1), (1, reps)))


def _stack_halves(q, half):
    lane = lax.broadcasted_iota(jnp.int32, q.shape, 1)
    zero = jnp.zeros_like(q)
    return jnp.concatenate([jnp.where(lane < half, q, zero), jnp.where(lane >= half, q, zero)], axis=0)


def _exp(x, sign=1.0):
    return jnp.exp2(x * (sign * LOG2_E))


def _softplus_parts(z):
    sp = jnp.maximum(z, 0.0) + jnp.log(1.0 + _exp(jnp.abs(z), -1.0))
    return sp, z - sp


def _split_bf16(x):
    hi = x.astype(BF16)
    return hi, (x - hi.astype(F32)).astype(BF16)


def _suffix_matrix(n, keys_on_rows):
    a = lax.broadcasted_iota(jnp.int32, (n, n), 0)
    b = lax.broadcasted_iota(jnp.int32, (n, n), 1)
    after = (b > a) if keys_on_rows else (a > b)
    return jnp.where(after, 1.0, 0.0).astype(BF16)


def _lambda_value(lq1, lk1, lq2, lk2):
    return (jnp.exp(jnp.sum(lq1[...] * lk1[...], axis=-1, keepdims=True))
            - jnp.exp(jnp.sum(lq2[...] * lk2[...], axis=-1, keepdims=True))
            + LAMBDA_INIT_LAYER0)


def _staggered(stages, n, order=None):
    order = order or [0] + list(range(len(stages) - 1, 0, -1))
    for t in range(n + len(stages) - 1):
        for s in order:
            if 0 <= t - s < n:
                stages[s](t - s)


def _softmax_weights_t(s, m, l, mask):
    if mask is not None:
        s = jnp.where(mask, s, NEG_INF)
    m_new = jnp.maximum(m, jnp.max(s, axis=0, keepdims=True))
    alpha = _exp(m - m_new)
    p = _exp(s - m_new)
    return m_new, alpha * l + jnp.sum(p, axis=0, keepdims=True), alpha, p.astype(BF16)


def _stick_parts_t(z, mask):
    sp, log_sig = _softplus_parts(z)
    if mask is not None:
        sp = jnp.where(mask, sp, 0.0)
    return log_sig, jnp.sum(sp, axis=0, keepdims=True), _split_bf16(sp)


def _stick_weights_t(log_sig, later, r, mask):
    a = _exp(log_sig - later - r)
    if mask is not None:
        a = jnp.where(mask, a, 0.0)
    return a.astype(BF16)


def _key_query_positions(n_keys, rows):
    s = lax.broadcasted_iota(jnp.int32, (n_keys, 2 * rows), 0)
    t = _mod_pow2(lax.broadcasted_iota(jnp.int32, (n_keys, 2 * rows), 1), rows)
    return s, t


def _diff_finish_t(acc, l, lam, g_col, rows):
    o = acc / l
    o = o[:, :rows] - lam * o[:, rows:]
    y = o * lax.rsqrt(jnp.mean(o * o, axis=0, keepdims=True) + NORM_EPS) * g_col
    return (y * (1.0 - LAMBDA_INIT_LAYER0)).T.astype(BF16)


def _stick_finish_t(acc, rows, half):
    row = lax.broadcasted_iota(jnp.int32, (LANES, rows), 0)
    return jnp.where(row < half, acc[:, :rows], acc[:, rows:]).T.astype(BF16)


def _attention_units(q_ref):
    return [(seq, g) for seq in range(q_ref.shape[0]) for g in range(q_ref.shape[2] // LANES)]


def _transpose_values(v_ref, vt_ref):
    nblk, _, n = vt_ref.shape
    for c in range(nblk):
        vt_ref[c] = v_ref[c * n:(c + 1) * n, :].astype(F32).T.astype(BF16)


def _diff_prompt_body(lq1, lk1, lq2, lk2, g_ref, q_ref, k_ref, v_ref, o_ref, vt_ref, acc_ref, *, rows, half):
    i = pl.program_id(1)
    units = _attention_units(q_ref)

    @pl.when(i == 0)
    def _():
        for seq in range(q_ref.shape[0]):
            _transpose_values(v_ref.at[seq], vt_ref.at[seq])

    qs = [_stack_halves(q_ref[seq, :, _lanes(g)], half) for seq, g in units]
    acc_ref[...] = jnp.zeros_like(acc_ref)

    def step(j, stats, mask):
        off = pl.multiple_of(j * rows, rows)
        scores, weights, out = {}, {}, {}

        def score(u):
            seq, g = units[u]
            scores[u] = _dot_nt(k_ref[seq, pl.ds(off, rows), _lanes(g)], qs[u])

        def weigh(u):
            m, l, alpha, p = _softmax_weights_t(scores.pop(u), *stats[u], mask)
            out[u] = (m, l)
            weights[u] = (alpha, p)

        def update(u):
            seq, g = units[u]
            alpha, p = weights.pop(u)
            acc_ref[u] = alpha * acc_ref[u] + _dot(vt_ref[seq, j, _lanes(g), :], p)

        _staggered([score, weigh, update], len(units))
        return tuple(out[u] for u in range(len(units)))

    init = tuple((jnp.full((1, 2 * rows), NEG_INF, F32), jnp.zeros((1, 2 * rows), F32)) for _ in units)
    stats = lax.fori_loop(0, i, lambda j, st: step(j, st, None), init)
    s, t = _key_query_positions(rows, rows)
    stats = step(i, stats, _div_pow2(s, CHUNK) <= _div_pow2(t, CHUNK))
    lam = _lambda_value(lq1, lk1, lq2, lk2)
    for u, (seq, g) in enumerate(units):
        o_ref[seq, :, _lanes(g)] = _diff_finish_t(acc_ref[u], stats[u][1], lam, g_ref[...], rows)


def _stick_prompt_body(q_ref, k_ref, v_ref, o_ref, vt_ref, acc_ref, *, rows, half):
    i = pl.program_id(1)
    units = _attention_units(q_ref)

    @pl.when(i == 0)
    def _():
        for seq in range(q_ref.shape[0]):
            _transpose_values(v_ref.at[seq], vt_ref.at[seq])

    qs = [_stack_halves(q_ref[seq, :, _lanes(g)], half) for seq, g in units]
    suffix = _suffix_matrix(rows, keys_on_rows=True)
    acc_ref[...] = jnp.zeros_like(acc_ref)

    def step(j, carry, mask):
        off = pl.multiple_of(j * rows, rows)
        logits, parts, sums, weights, out = {}, {}, {}, {}, {}

        def logit(u):
            seq, g = units[u]
            logits[u] = _dot_nt(k_ref[seq, pl.ds(off, rows), _lanes(g)], qs[u])

        def split(u):
            log_sig, total, parts[u] = _stick_parts_t(logits.pop(u), mask)
            out[u] = carry[u] + total
            sums[u] = log_sig

        def later(u):
            hi, lo = parts.pop(u)
            sums[u] = (sums[u], _dot(suffix, hi) + _dot(suffix, lo))

        def weigh(u):
            weights[u] = _stick_weights_t(*sums.pop(u), carry[u], mask)

        def update(u):
            seq, g = units[u]
            acc_ref[u] += _dot(vt_ref[seq, j, _lanes(g), :], weights.pop(u))

        _staggered([logit, split, later, weigh, update], len(units))
        return tuple(out[u] for u in range(len(units)))

    def live(carry):
        return jnp.min(functools.reduce(jnp.minimum, carry)) < DEAD_SOFTPLUS_SUM

    def older_block(state):
        jj, _, carry = state
        carry = step(i - 1 - jj, carry, None)
        return jj + 1, live(carry), carry

    s, t = _key_query_positions(rows, rows)
    carry = step(i, tuple(jnp.zeros((1, 2 * rows), F32) for _ in units), s < t)
    lax.while_loop(lambda state: (state[0] < i) & state[1], older_block, (jnp.int32(0), live(carry), carry))
    for u, (seq, g) in enumerate(units):
        o_ref[seq, :, _lanes(g)] = _stick_finish_t(acc_ref[u], rows, half)


def _prompt_attention(body, name, extra_inputs, extra_specs, q, k, v, *, batch, seq, rows, half):
    n, width = q.shape
    body = functools.partial(body, rows=rows, half=half)
    nq = seq // rows
    together = ATTN_SEQUENCES * STICK_ROWS // rows
    by_seq = lambda a: a.reshape(batch, seq, width)
    q_spec = pl.BlockSpec((together, rows, width), lambda b, i: (b, i, 0))
    kv_spec = pl.BlockSpec((together, seq, width), lambda b, i: (b, 0, 0))
    return pl.pallas_call(
        body,
        grid=(batch // together, nq),
        in_specs=extra_specs + [q_spec, kv_spec, kv_spec],
        out_specs=q_spec,
        out_shape=jax.ShapeDtypeStruct((batch, seq, width), BF16),
        scratch_shapes=[pltpu.VMEM((together, nq, width, rows), BF16),
                        pltpu.VMEM((together * (width // LANES), LANES, 2 * rows), F32)],
        compiler_params=_params(2),
        name=name,
    )(*extra_inputs, by_seq(q), by_seq(k), by_seq(v)).reshape(n, width)


def _pad_rows(x, n):
    return jnp.concatenate([x, jnp.zeros((n - x.shape[0], x.shape[1]), x.dtype)], axis=0)


def _stacked_positions(rows, n_keys):
    t = _mod_pow2(lax.broadcasted_iota(jnp.int32, (2 * rows, n_keys), 0), rows)
    s = lax.broadcasted_iota(jnp.int32, (2 * rows, n_keys), 1)
    return t, s


def _diff_decode_body(lq1, lk1, lq2, lk2, g_ref, q_ref, kn_ref, vn_ref, ckt_ref, cv_ref, o_ref,
                      *, rows, half):
    past = ckt_ref.shape[1]
    lam = _lambda_value(lq1, lk1, lq2, lk2)
    t, s = _stacked_positions(rows, LANES)
    new_mask = (s < rows) & (_div_pow2(past + s, CHUNK) <= _div_pow2(past + t, CHUNK))
    groups = q_ref.shape[1] // LANES
    scores, weights = {}, {}

    def score(g):
        qs = _stack_halves(q_ref[:, _lanes(g)], half)
        scores[g] = (_dot(qs, ckt_ref[_lanes(g), :].astype(BF16)),
                     _dot_nt(qs, _pad_rows(kn_ref[:, _lanes(g)], LANES)))

    def weigh(g):
        s_c, s_n = scores.pop(g)
        s_n = jnp.where(new_mask, s_n, NEG_INF)
        m = jnp.maximum(jnp.max(s_c, axis=1, keepdims=True), jnp.max(s_n, axis=1, keepdims=True))
        p_c = _exp(s_c - m)
        p_n = _exp(s_n - m)
        l = jnp.sum(p_c, axis=1, keepdims=True) + jnp.sum(p_n, axis=1, keepdims=True)
        weights[g] = (p_c.astype(BF16), p_n.astype(BF16), l)

    def update(g):
        p_c, p_n, l = weights.pop(g)
        v_c = cv_ref[pl.ds(g, past, stride=groups), :]
        o = (_dot(p_c, v_c.astype(BF16)) + _dot(p_n, _pad_rows(vn_ref[:, _lanes(g)], LANES))) / l
        o = o[:rows] - lam * o[rows:]
        o_ref[:, _lanes(g)] = (_rms(o, g_ref[...]) * (1.0 - LAMBDA_INIT_LAYER0)).astype(BF16)

    _staggered([score, weigh, update], groups)


def _stick_decode_body(q_ref, kn_ref, vn_ref, ckt_ref, cvt_ref, o_ref, *, rows, half):
    past = ckt_ref.shape[1]
    suffix = _suffix_matrix(DECODE_KEYS, keys_on_rows=False)
    suffix_new = _suffix_matrix(LANES, keys_on_rows=False)
    t, s = _stacked_positions(rows, LANES)
    new_mask = (s < rows) & (s < t)
    lane = lax.broadcasted_iota(jnp.int32, (rows, LANES), 1)
    blocks = [slice(b * DECODE_KEYS, (b + 1) * DECODE_KEYS) for b in range(past // DECODE_KEYS)]
    logits, parts, laters, weights = {}, {}, {}, {}

    def logit(g):
        qs = _stack_halves(q_ref[:, _lanes(g)], half)
        logits[g] = (_dot_nt(qs, _pad_rows(kn_ref[:, _lanes(g)], LANES)),
                     _dot(qs, ckt_ref[_lanes(g), :].astype(BF16)))

    def split(g):
        z_n, z_c = logits.pop(g)
        sp_n, log_sig_n = _softplus_parts(z_n)
        sp_n = jnp.where(new_mask, sp_n, 0.0)
        sp_c, log_sig_c = _softplus_parts(z_c)
        carries = [jnp.sum(sp_n, axis=1, keepdims=True)]
        for cols in reversed(blocks[1:]):
            carries.append(carries[-1] + jnp.sum(sp_c[:, cols], axis=1, keepdims=True))
        parts[g] = (_split_bf16(sp_n), _split_bf16(sp_c))
        laters[g] = (log_sig_n, log_sig_c, carries[::-1])

    def later(g):
        (hi_n, lo_n), (hi_c, lo_c) = parts.pop(g)
        laters[g] += (_dot(hi_n, suffix_new) + _dot(lo_n, suffix_new),
                      [_dot(hi_c[:, cols], suffix) + _dot(lo_c[:, cols], suffix) for cols in blocks])

    def weigh(g):
        log_sig_n, log_sig_c, carries, later_n, later_c = laters.pop(g)
        a_n = jnp.where(new_mask, _exp(log_sig_n - later_n), 0.0).astype(BF16)
        weights[g] = (a_n, [_exp(log_sig_c[:, cols] - later_c[b] - carries[b]).astype(BF16)
                            for b, cols in enumerate(blocks)])

    def update(g):
        a_n, a_c = weights.pop(g)
        vt = cvt_ref[_lanes(g), :].astype(BF16)
        acc = _dot(a_n, _pad_rows(vn_ref[:, _lanes(g)], LANES))
        for b, cols in enumerate(blocks):
            acc = acc + _dot_nt(a_c[b], vt[:, cols])
        o_ref[:, _lanes(g)] = jnp.where(lane < half, acc[:rows], acc[rows:]).astype(BF16)

    _staggered([logit, split, later, weigh, update], q_ref.shape[1] // LANES)


def _decode_body(*refs, n_diff, rows, half):
    diff_in, stick_in, (oa_ref, ob_ref) = refs[:n_diff], refs[n_diff:-2], refs[-2:]
    _diff_decode_body(*diff_in, oa_ref, rows=rows, half=half)
    _stick_decode_body(*stick_in, ob_ref, rows=rows, half=half)


def _decode_attention(extra_inputs, extra_specs, diff_operands, stick_operands, *, rows, half):
    n, width = diff_operands[0].shape
    new_spec = pl.BlockSpec((rows, width), lambda b: (b, 0))
    cache_spec = lambda c: pl.BlockSpec((None,) + c.shape[1:], lambda b: (b,) + (0,) * (c.ndim - 1))
    operand_specs = lambda ops: [new_spec] * 3 + [cache_spec(c) for c in ops[3:]]
    out = jax.ShapeDtypeStruct((n, width), BF16)
    return pl.pallas_call(
        functools.partial(_decode_body, n_diff=len(extra_inputs) + len(diff_operands), rows=rows, half=half),
        grid=(n // rows,),
        in_specs=extra_specs + operand_specs(diff_operands) + operand_specs(stick_operands),
        out_specs=[new_spec, new_spec],
        out_shape=[out, out],
        compiler_params=_params(1),
        name="decode_attention",
    )(*extra_inputs, *diff_operands, *stick_operands)


def _merge_ffn_body(*refs, d_ff, seq_rows):
    (x_ref, oa_ref, ob_ref, ga_ref, gb_ref, wa_ref, wb_ref, wo_ref, fg_ref, wup_ref, cw_ref, cb_ref,
     wdn_ref, ng_ref) = refs[:14]
    if seq_rows is None:
        y_ref, conv_ref, acc_ref, carry_ref = refs[14:]
    else:
        h0_ref, h1_ref, y_ref, last2_ref, last1_ref, acc_ref, u_ref = refs[14:]
    d_model = x_ref.shape[1]
    parts = acc_ref.shape[0]
    tm = x_ref.shape[0] // parts
    block = lambda p: slice(p * tm, (p + 1) * tm)
    assert cw_ref.shape[0] == 3

    if seq_rows is None:
        n_slabs = tm // 8
        interleave = lambda a: jnp.swapaxes(a.reshape(8, n_slabs, d_model), 0, 1).reshape(tm, d_model)
        position_order = lambda a: jnp.swapaxes(a.reshape(n_slabs, 8, d_model), 0, 1).reshape(tm, d_model)

        @pl.when(pl.program_id(1) == 0)
        def _():
            carry_ref[...] = jnp.zeros_like(carry_ref)
        sublane = lax.broadcasted_iota(jnp.int32, (8, FFN_COLS), 0)
    else:
        interleave = position_order = lambda a: a
        pos = _mod_pow2(lax.broadcasted_iota(jnp.int32, (tm, FFN_COLS), 0), seq_rows)

    branches, mixes, projected, x1, xn = {}, {}, {}, {}, {}

    def branch(p):
        branches[p] = _dot(oa_ref[block(p), :], wa_ref[...]), _dot(ob_ref[block(p), :], wb_ref[...])

    def mixed(p):
        bra, brb = branches.pop(p)
        mixes[p] = (ga_ref[block(p), :].astype(F32) * bra + gb_ref[block(p), :].astype(F32) * brb).astype(BF16)

    def project(p):
        projected[p] = _dot(mixes.pop(p), wo_ref[...])

    def residual(p):
        x1[p] = interleave(x_ref[block(p), :] + projected.pop(p))
        xn[p] = _rms(x1[p], fg_ref[...]).astype(BF16)

    def finish(p):
        y_ref[block(p), :] = position_order(_rms(x1[p] + acc_ref[p], ng_ref[...]))

    def taps(u, u1, u2, cols):
        return cb_ref[:, cols] + cw_ref[0:1, cols] * u2 + cw_ref[1:2, cols] * u1 + cw_ref[2:3, cols] * u

    def conv(u, cols):
        if seq_rows is None:
            prev2 = carry_ref[7:8, cols]
            prev1 = carry_ref[15:16, cols]
            carry_ref[:, cols] = u[tm - 16:, :]
            conv_ref[:, cols] = u[tm - 16:, :]
            before1 = jnp.where(sublane == 0, prev1, pltpu.roll(u[tm - 8:, :], 1, 0))
            before2 = jnp.where(sublane == 0, prev2, pltpu.roll(u[tm - 16:tm - 8, :], 1, 0))
            u1 = jnp.concatenate([before1, u[:tm - 8, :]], axis=0)
            u2 = jnp.concatenate([before2, before1, u[:tm - 16, :]], axis=0)
            return taps(u, u1, u2, cols)
        else:
            n_seq = tm // seq_rows
            expand = lambda h: jnp.broadcast_to(h[:, None, :], (n_seq, seq_rows, FFN_COLS)).reshape(tm, FFN_COLS)
            prev2 = expand(h0_ref[:, cols])
            prev1 = expand(h1_ref[:, cols])
            for part in range(FFN_COLS // LANES):
                lanes = _lanes(part)
                out = slice(cols.start + part * LANES, cols.start + (part + 1) * LANES)
                u_ref[...] = u[:, lanes]
                last2_ref[:, out] = u_ref[pl.ds(seq_rows - 2, n_seq, stride=seq_rows), :]
                last1_ref[:, out] = u_ref[pl.ds(seq_rows - 1, n_seq, stride=seq_rows), :]
            u1 = jnp.where(pos == 0, prev1, pltpu.roll(u, 1, 0))
            u2 = jnp.where(pos == 0, prev2, jnp.where(pos == 1, prev1, pltpu.roll(u, 2, 0)))
            return taps(u, u1, u2, cols)

    chunks = d_ff // FFN_COLS
    gate_cols = lambda c: slice(c * FFN_COLS, (c + 1) * FFN_COLS)
    val_cols = lambda c: slice(d_ff + c * FFN_COLS, d_ff + (c + 1) * FFN_COLS)
    ups, hidden = {}, {}

    def up(i):
        p, c = divmod(i, chunks)
        ups[i] = _dot(xn[p], wup_ref[:, gate_cols(c)]), _dot(xn[p], wup_ref[:, val_cols(c)])

    def gated(i):
        c = i % chunks
        ug, uv = ups.pop(i)
        gate = conv(ug, gate_cols(c))
        hidden[i] = (gate * conv(uv, val_cols(c)) / (1.0 + _exp(gate, -1.0))).astype(BF16)

    def down(i):
        p, c = divmod(i, chunks)
        part = _dot(hidden.pop(i), wdn_ref[gate_cols(c), :])
        acc_ref[p] = part if c == 0 else acc_ref[p] + part

    merge = [branch, mixed, project, residual]
    for step in merge:
        step(0)
    for t in range(parts * chunks + 2):
        for stage, fn in ((0, up), (2, down), (1, gated)):
            if 0 <= t - stage < parts * chunks:
                fn(t - stage)
        p, c = divmod(t, chunks)
        if p + 1 < parts and 1 <= c <= len(merge):
            merge[c - 1](p + 1)
        if t >= 2 and (t - 2) % chunks == chunks - 1:
            finish((t - 2) // chunks)


def _merge_ffn(x2d, oa, ob, ga, gb, wa, wb, wo, fg, wup, cw, cb, wdn, ng, *, batch, history=None):
    n, d_model = x2d.shape
    width = oa.shape[1]
    d_ff = wdn.shape[0]
    two_ff = wup.shape[1]
    seq = n // batch
    consts = [wa, wb, wo, fg, wup, cw, cb, wdn, ng]
    const_specs = [_const_spec(a.shape) for a in consts]
    if history is None:
        tm = min(FFN_ROWS, seq)
        nt = seq // tm
        grid = (batch, nt)
        row = lambda b, j: (b * nt + j, 0)
        extra, extra_specs = [], []
        conv_shape = [jax.ShapeDtypeStruct((batch, 16, two_ff), F32)]
        conv_spec = [pl.BlockSpec((None, 16, two_ff), lambda b, j: (b, 0, 0))]
        scratch = [pltpu.VMEM((FFN_BLOCKS, tm // FFN_BLOCKS, d_model), F32), pltpu.VMEM((16, two_ff), F32)]
        seq_rows = None
    else:
        tm = n
        grid = (1,)
        row = lambda i: (0, 0)
        extra = list(history)
        extra_specs = [_const_spec(h.shape) for h in history]
        conv_shape = [jax.ShapeDtypeStruct((batch, two_ff), F32)] * 2
        conv_spec = [pl.BlockSpec((batch, two_ff), row)] * 2
        scratch = [pltpu.VMEM((1, tm, d_model), F32), pltpu.VMEM((tm, LANES), F32)]
        seq_rows = seq
    tile = lambda cols: pl.BlockSpec((tm, cols), row)
    return pl.pallas_call(
        functools.partial(_merge_ffn_body, d_ff=d_ff, seq_rows=seq_rows),
        grid=grid,
        in_specs=[tile(d_model), tile(width), tile(width), tile(d_model), tile(d_model)]
                 + const_specs + extra_specs,
        out_specs=[tile(d_model)] + conv_spec,
        out_shape=[jax.ShapeDtypeStruct((n, d_model), F32)] + conv_shape,
        scratch_shapes=scratch,
        compiler_params=_params(len(grid)),
        name="merge_ffn" if history is None else "merge_ffn_streams",
    )(x2d, oa, ob, ga, gb, *consts, *extra)


def kernel(x_prompt, x_sample, cache_diff_k, cache_diff_v, cache_sb_k, cache_sb_v, state_conv,
           attn_norm_g, w_in, lambda_q1, lambda_k1, lambda_q2, lambda_k2, subln_g, w_branch_a,
           w_branch_b, w_out, ffn_norm_g, w_up, conv_w, conv_b, w_down, final_norm_g):
    depth = w_in.shape[0]
    assert depth == 1, "single-layer trunk only"
    batch, seq, d_model = x_prompt.shape
    dec_batch, dec_seq, _ = x_sample.shape
    past = cache_diff_k.shape[2]
    width = d_model // 2
    da_heads, _, da_dim = cache_diff_k.shape[3:]
    da_v = cache_diff_v.shape[-1]
    sb_heads, sb_dim = cache_sb_k.shape[3:]
    assert da_dim == sb_dim and 2 * da_dim == LANES and da_v == LANES
    half = da_dim

    bf = lambda w: w[0].astype(BF16)
    row1 = lambda v: v.reshape(1, -1)
    w_in_bf = bf(w_in)
    consts = (bf(w_branch_a), bf(w_branch_b), bf(w_out), ffn_norm_g, bf(w_up), conv_w[0], conv_b,
              bf(w_down), row1(final_norm_g))
    lams = [row1(v[0]) for v in (lambda_q1, lambda_k1, lambda_q2, lambda_k2)]
    g_row, g_col = row1(subln_g[0]), subln_g[0].reshape(-1, 1)
    specs = lambda arrays: [_const_spec(a.shape) for a in arrays]

    xp = x_prompt.reshape(batch * seq, d_model)
    cos_p, sin_p = _rope_tables(jnp.arange(seq), da_dim)
    (p_dk, p_dv, p_sk, p_sv, qd, kd, vd, qs, ks, vs, ga, gb) = _in_proj(
        xp, attn_norm_g, w_in_bf, cos_p, sin_p, head_dim=da_dim, seq=seq)
    major = lambda c, *dims: jnp.moveaxis(c.reshape(batch, *dims, seq), -1, 1)
    p_dk, p_sk, p_sv = (major(p_dk, da_heads, 2, da_dim), major(p_sk, sb_heads, sb_dim),
                        major(p_sv, sb_heads, sb_dim))
    oa = _prompt_attention(_diff_prompt_body, "diff_prompt", lams + [g_col], specs(lams + [g_col]), qd, kd, vd,
                           batch=batch, seq=seq, rows=DIFF_ROWS, half=half)
    ob = _prompt_attention(_stick_prompt_body, "stick_prompt", [], [], qs, ks, vs,
                           batch=batch, seq=seq, rows=STICK_ROWS, half=half)
    y_p, conv_p = _merge_ffn(xp, oa, ob, ga, gb, *consts, batch=batch)

    xs = x_sample.reshape(dec_batch * dec_seq, d_model)
    cos_s, sin_s = _rope_tables(past + jnp.arange(dec_seq), da_dim)
    cos_s, sin_s = jnp.tile(cos_s, (dec_batch, 1)), jnp.tile(sin_s, (dec_batch, 1))
    (s_dk, s_dv, s_sk, s_sv, qd, kd, vd, qs, ks, vs, ga, gb) = _in_proj(
        xs, attn_norm_g, w_in_bf, cos_s, sin_s, head_dim=da_dim)
    minor = lambda c: jnp.moveaxis(c[0], 1, -1).reshape(dec_batch, width, past)
    oa, ob = _decode_attention(
        lams + [g_row], specs(lams + [g_row]),
        [qd, kd, vd, minor(cache_diff_k), cache_diff_v[0].reshape(dec_batch, past * da_heads, da_v)],
        [qs, ks, vs, minor(cache_sb_k), minor(cache_sb_v)], rows=dec_seq, half=half)
    y_s, last2, last1 = _merge_ffn(xs, oa, ob, ga, gb, *consts, batch=dec_batch,
                                   history=(state_conv[0, :, 0], state_conv[0, :, 1]))

    conv_p = jnp.stack([conv_p[:, 7], conv_p[:, 15]], axis=1)
    conv_s = jnp.stack([last2, last1], axis=1)
    return (y_p.reshape(batch, seq, d_model), y_s.reshape(dec_batch, dec_seq, d_model),
            p_dk[None], p_dv.reshape(depth, batch, seq, da_heads, da_v), p_sk[None], p_sv[None],
            conv_p[None],
            s_dk.reshape(depth, dec_batch, dec_seq, da_heads, 2, da_dim),
            s_dv.reshape(depth, dec_batch, dec_seq, da_heads, da_v),
            s_sk.reshape(depth, dec_batch, dec_seq, sb_heads, sb_dim),
            s_sv.reshape(depth, dec_batch, dec_seq, sb_heads, sb_dim),
            conv_s[None])
```

```python
import functools

import jax
import jax.numpy as jnp
from jax import lax
from jax.experimental import pallas as pl
from jax.experimental.pallas import tpu as pltpu

F32 = jnp.float32
BF16 = jnp.bfloat16

CHUNK = 64
ROPE_THETA = 10000.0
NORM_EPS = 1e-6
NEG_INF = -1e30
LAMBDA_INIT_LAYER0 = 0.8 - 0.6 * 1.0
LOG2_E = 1.4426950408889634
DEAD_SOFTPLUS_SUM = 104.0

LANES = 128
VMEM_LIMIT_BYTES = 56 * 1024 * 1024

PROJ_ROWS = 512
PROJ_BLOCKS = 2
DIFF_ROWS = 512
STICK_ROWS = 256
ATTN_SEQUENCES = 2
FFN_ROWS = 512
FFN_COLS = 256
FFN_BLOCKS = 2
DECODE_KEYS = 256


def _rms(x, g):
    return x * lax.rsqrt(jnp.mean(x * x, axis=-1, keepdims=True) + NORM_EPS) * g


def _dot(a, b):
    return jnp.dot(a, b, preferred_element_type=F32)


def _dot_nt(a, b):
    return lax.dot_general(a, b, (((1,), (1,)), ((), ())), preferred_element_type=F32)


def _dot_tn(a, b):
    return lax.dot_general(a, b, (((0,), (0,)), ((), ())), preferred_element_type=F32)


def _log2(n):
    assert n > 0 and n & (n - 1) == 0, n
    return n.bit_length() - 1


def _mod_pow2(x, n):
    _log2(n)
    return jnp.bitwise_and(x, n - 1)


def _div_pow2(x, n):
    return jnp.right_shift(x, _log2(n))


def _lanes(g):
    return slice(g * LANES, (g + 1) * LANES)


def _const_spec(shape):
    return pl.BlockSpec(shape, lambda *_: (0,) * len(shape), pipeline_mode=pl.Buffered(1))


def _params(n_axes, **flags):
    return pltpu.CompilerParams(dimension_semantics=("arbitrary",) * n_axes,
                                vmem_limit_bytes=VMEM_LIMIT_BYTES, flags=flags or None)


def _in_proj_body(x_ref, g_ref, w_ref, cos_ref, sin_ref,
                  kd_ref, vd_ref, ks_ref, vs_ref,
                  qd_bf, kd_bf, vd_bf, qs_bf, ks_bf, vs_bf, ga_ref, gb_ref,
                  *, width, d_model, head_dim, q_scale, position_minor):
    block_rows = x_ref.shape[0] // PROJ_BLOCKS
    xb = {}

    def norm(p):
        rows = slice(p * block_rows, (p + 1) * block_rows)
        xb[p] = _rms(x_ref[rows, :], g_ref[...]).astype(BF16)

    def put(ref, c, z, rows):
        if position_minor:
            ref[_lanes(c), rows] = z.T
        else:
            ref[rows, _lanes(c)] = z

    lane = lax.broadcasted_iota(jnp.int32, (block_rows, LANES), 1)
    half = head_dim // 2
    first_half = _mod_pow2(lane, head_dim) < half

    def rope(z, rows):
        rot = jnp.where(first_half, pltpu.roll(z, LANES - half, 1), pltpu.roll(z, half, 1))
        return z * cos_ref[rows, :] + rot * sin_ref[rows, :]

    tiles = range(width // LANES)

    def diff_q(z, rows):
        for c in tiles:
            qd_bf[rows, _lanes(c)] = (rope(z[:, _lanes(c)], rows) * q_scale).astype(BF16)

    def diff_k(z, rows):
        for c in tiles:
            rk = rope(z[:, _lanes(c)], rows)
            put(kd_ref, c, rk, rows)
            kd_bf[rows, _lanes(c)] = rk.astype(BF16)

    def diff_v(z, rows):
        vd_bf[rows, :] = z.astype(BF16)
        if position_minor:
            for h in tiles:
                vd_ref[pl.ds(rows.start * len(tiles) + h, block_rows, stride=len(tiles)), :] = z[:, _lanes(h)]
        else:
            vd_ref[rows, :] = z

    def stick_q(z, rows):
        qs_bf[rows, :] = (z * q_scale).astype(BF16)

    def stick_kv(out_ref, bf_ref):
        def post(z, rows):
            bf_ref[rows, :] = z.astype(BF16)
            for c in tiles:
                put(out_ref, c, z[:, _lanes(c)], rows)
        return post

    def gate(out_ref, c):
        def post(z, rows):
            out_ref[rows, c * width:(c + 1) * width] = (1.0 / (1.0 + _exp(z, -1.0))).astype(BF16)
        return post

    posts = [diff_q, diff_k, diff_v, stick_q, stick_kv(ks_ref, ks_bf), stick_kv(vs_ref, vs_bf)]
    posts += [gate(ref, c) for ref in (ga_ref, gb_ref) for c in range(d_model // width)]
    units = len(posts)
    z = {}

    def matmul(i):
        p, u = divmod(i, units)
        z[i] = _dot(xb[p], w_ref[:, u * width:(u + 1) * width])

    def post(i):
        p, u = divmod(i, units)
        posts[u](z.pop(i), slice(p * block_rows, (p + 1) * block_rows))

    norm(0)
    for t in range(PROJ_BLOCKS * units + 1):
        if t < PROJ_BLOCKS * units:
            matmul(t)
        if t >= 1:
            post(t - 1)
        if t % units == 1 and t // units + 1 < PROJ_BLOCKS:
            norm(t // units + 1)


def _in_proj(x2d, g, w_bf, cos_t, sin_t, *, head_dim, seq=None):
    n, d_model = x2d.shape
    n_in = w_bf.shape[1]
    width = d_model // 2
    tm = min(PROJ_ROWS, n)
    n_tab = cos_t.shape[0] // tm
    row = lambda i: (i, 0)
    tab = lambda i: (i % n_tab, 0)
    row_spec = pl.BlockSpec((tm, width), row)
    bf_out = jax.ShapeDtypeStruct((n, width), BF16)
    gate_out = jax.ShapeDtypeStruct((n, d_model), BF16)
    if seq is None:
        cache_specs = [row_spec] * 4
        cache_shapes = [jax.ShapeDtypeStruct((n, width), F32)] * 4
    else:
        nt = seq // tm
        minor_spec = pl.BlockSpec((None, width, tm), lambda i: (i // nt, 0, i % nt))
        minor_shape = jax.ShapeDtypeStruct((n // seq, width, seq), F32)
        heads = width // LANES
        cache_specs = [minor_spec, pl.BlockSpec((tm * heads, LANES), row), minor_spec, minor_spec]
        cache_shapes = [minor_shape, jax.ShapeDtypeStruct((n * heads, LANES), F32), minor_shape, minor_shape]
    body = functools.partial(_in_proj_body, width=width, d_model=d_model, head_dim=head_dim,
                             q_scale=head_dim ** -0.5, position_minor=seq is not None)
    return pl.pallas_call(
        body,
        grid=(n // tm,),
        in_specs=[pl.BlockSpec((tm, d_model), row),
                  _const_spec((1, d_model)),
                  _const_spec((d_model, n_in)),
                  pl.BlockSpec((tm, LANES), tab),
                  pl.BlockSpec((tm, LANES), tab)],
        out_specs=cache_specs + [row_spec] * 6 + [pl.BlockSpec((tm, d_model), row)] * 2,
        out_shape=cache_shapes + [bf_out] * 6 + [gate_out] * 2,
        compiler_params=_params(1),
        name="in_proj",
    )(x2d, g, w_bf, cos_t, sin_t)


def _rope_tables(pos, head_dim):
    half = head_dim // 2
    inv = ROPE_THETA ** (-jnp.arange(half, dtype=F32) * 2.0 / head_dim)
    ang = pos.astype(F32)[:, None] * inv[None, :]
    cos = jnp.cos(ang)
    sin = jnp.sin(ang)
    reps = LANES // head_dim
    return (jnp.tile(jnp.concatenate([cos, cos], -1), (1, reps)),
            jnp.tile(jnp.concatenate([-sin, sin], -1), (1, reps)))


def _stack_halves(q, half):
    lane = lax.broadcasted_iota(jnp.int32, q.shape, 1)
    zero = jnp.zeros_like(q)
    return jnp.concatenate([jnp.where(lane < half, q, zero), jnp.where(lane >= half, q, zero)], axis=0)


def _exp(x, sign=1.0):
    return jnp.exp2(x * (sign * LOG2_E))


def _softplus_parts(z):
    sp = jnp.maximum(z, 0.0) + jnp.log(1.0 + _exp(jnp.abs(z), -1.0))
    return sp, z - sp


def _split_bf16(x):
    hi = x.astype(BF16)
    return hi, (x - hi.astype(F32)).astype(BF16)


def _suffix_matrix(n, keys_on_rows):
    a = lax.broadcasted_iota(jnp.int32, (n, n), 0)
    b = lax.broadcasted_iota(jnp.int32, (n, n), 1)
    after = (b > a) if keys_on_rows else (a > b)
    return jnp.where(after, 1.0, 0.0).astype(BF16)


def _lambda_value(lq1, lk1, lq2, lk2):
    return (jnp.exp(jnp.sum(lq1[...] * lk1[...], axis=-1, keepdims=True))
            - jnp.exp(jnp.sum(lq2[...] * lk2[...], axis=-1, keepdims=True))
            + LAMBDA_INIT_LAYER0)


def _staggered(stages, n, order=None):
    order = order or [0] + list(range(len(stages) - 1, 0, -1))
    for t in range(n + len(stages) - 1):
        for s in order:
            if 0 <= t - s < n:
                stages[s](t - s)


def _softmax_weights_t(s, m, l, mask):
    if mask is not None:
        s = jnp.where(mask, s, NEG_INF)
    m_new = jnp.maximum(m, jnp.max(s, axis=0, keepdims=True))
    alpha = _exp(m - m_new)
    p = _exp(s - m_new)
    return m_new, alpha * l + jnp.sum(p, axis=0, keepdims=True), alpha, p.astype(BF16)


def _stick_parts_t(z, mask):
    sp, log_sig = _softplus_parts(z)
    if mask is not None:
        sp = jnp.where(mask, sp, 0.0)
    return log_sig, jnp.sum(sp, axis=0, keepdims=True), _split_bf16(sp)


def _stick_weights_t(log_sig, later, r, mask):
    a = _exp(log_sig - later - r)
    if mask is not None:
        a = jnp.where(mask, a, 0.0)
    return a.astype(BF16)


def _key_query_positions(n_keys, rows):
    s = lax.broadcasted_iota(jnp.int32, (n_keys, 2 * rows), 0)
    t = _mod_pow2(lax.broadcasted_iota(jnp.int32, (n_keys, 2 * rows), 1), rows)
    return s, t


def _diff_finish_t(acc, l, lam, g_col, rows):
    o = acc / l
    o = o[:, :rows] - lam * o[:, rows:]
    y = o * lax.rsqrt(jnp.mean(o * o, axis=0, keepdims=True) + NORM_EPS) * g_col
    return (y * (1.0 - LAMBDA_INIT_LAYER0)).T.astype(BF16)


def _stick_finish_t(acc, rows, half):
    row = lax.broadcasted_iota(jnp.int32, (LANES, rows), 0)
    return jnp.where(row < half, acc[:, :rows], acc[:, rows:]).T.astype(BF16)


def _attention_units(q_ref):
    return [(seq, g) for seq in range(q_ref.shape[0]) for g in range(q_ref.shape[2] // LANES)]


def _diff_prompt_body(lq1, lk1, lq2, lk2, g_ref, q_ref, k_ref, v_ref, o_ref, acc_ref, *, rows, half):
    i = pl.program_id(1)
    units = _attention_units(q_ref)

    qs = [_stack_halves(q_ref[seq, :, _lanes(g)], half) for seq, g in units]
    acc_ref[...] = jnp.zeros_like(acc_ref)

    def step(j, stats, mask):
        off = pl.multiple_of(j * rows, rows)
        scores, weights, out = {}, {}, {}

        def score(u):
            seq, g = units[u]
            scores[u] = _dot_nt(k_ref[seq, pl.ds(off, rows), _lanes(g)], qs[u])

        def weigh(u):
            m, l, alpha, p = _softmax_weights_t(scores.pop(u), *stats[u], mask)
            out[u] = (m, l)
            weights[u] = (alpha, p)

        def update(u):
            seq, g = units[u]
            alpha, p = weights.pop(u)
            acc_ref[u] = alpha * acc_ref[u] + _dot_tn(v_ref[seq, pl.ds(off, rows), _lanes(g)], p)

        _staggered([score, weigh, update], len(units))
        return tuple(out[u] for u in range(len(units)))

    init = tuple((jnp.full((1, 2 * rows), NEG_INF, F32), jnp.zeros((1, 2 * rows), F32)) for _ in units)
    stats = lax.fori_loop(0, i, lambda j, st: step(j, st, None), init)
    s, t = _key_query_positions(rows, rows)
    stats = step(i, stats, _div_pow2(s, CHUNK) <= _div_pow2(t, CHUNK))
    lam = _lambda_value(lq1, lk1, lq2, lk2)
    for u, (seq, g) in enumerate(units):
        o_ref[seq, :, _lanes(g)] = _diff_finish_t(acc_ref[u], stats[u][1], lam, g_ref[...], rows)


def _stick_prompt_body(q_ref, k_ref, v_ref, o_ref, acc_ref, *, rows, half):
    i = pl.program_id(1)
    units = _attention_units(q_ref)

    qs = [_stack_halves(q_ref[seq, :, _lanes(g)], half) for seq, g in units]
    suffix = _suffix_matrix(rows, keys_on_rows=True)
    acc_ref[...] = jnp.zeros_like(acc_ref)

    def step(j, carry, mask):
        off = pl.multiple_of(j * rows, rows)
        logits, parts, sums, weights, out = {}, {}, {}, {}, {}

        def logit(u):
            seq, g = units[u]
            logits[u] = _dot_nt(k_ref[seq, pl.ds(off, rows), _lanes(g)], qs[u])

        def split(u):
            log_sig, total, parts[u] = _stick_parts_t(logits.pop(u), mask)
            out[u] = carry[u] + total
            sums[u] = log_sig

        def later(u):
            hi, lo = parts.pop(u)
            sums[u] = (sums[u], _dot(suffix, hi) + _dot(suffix, lo))

        def weigh(u):
            weights[u] = _stick_weights_t(*sums.pop(u), carry[u], mask)

        def update(u):
            seq, g = units[u]
            acc_ref[u] += _dot_tn(v_ref[seq, pl.ds(off, rows), _lanes(g)], weights.pop(u))

        _staggered([logit, split, later, weigh, update], len(units))
        return tuple(out[u] for u in range(len(units)))

    def live(carry):
        return jnp.min(functools.reduce(jnp.minimum, carry)) < DEAD_SOFTPLUS_SUM

    def older_block(state):
        jj, _, carry = state
        carry = step(i - 1 - jj, carry, None)
        return jj + 1, live(carry), carry

    s, t = _key_query_positions(rows, rows)
    carry = step(i, tuple(jnp.zeros((1, 2 * rows), F32) for _ in units), s < t)
    lax.while_loop(lambda state: (state[0] < i) & state[1], older_block, (jnp.int32(0), live(carry), carry))
    for u, (seq, g) in enumerate(units):
        o_ref[seq, :, _lanes(g)] = _stick_finish_t(acc_ref[u], rows, half)


def _prompt_attention(body, name, extra_inputs, extra_specs, q, k, v, *, batch, seq, rows, half):
    n, width = q.shape
    body = functools.partial(body, rows=rows, half=half)
    nq = seq // rows
    together = ATTN_SEQUENCES * STICK_ROWS // rows
    by_seq = lambda a: a.reshape(batch, seq, width)
    q_spec = pl.BlockSpec((together, rows, width), lambda b, i: (b, i, 0))
    kv_spec = pl.BlockSpec((together, seq, width), lambda b, i: (b, 0, 0))
    return pl.pallas_call(
        body,
        grid=(batch // together, nq),
        in_specs=extra_specs + [q_spec, kv_spec, kv_spec],
        out_specs=q_spec,
        out_shape=jax.ShapeDtypeStruct((batch, seq, width), BF16),
        scratch_shapes=[pltpu.VMEM((together * (width // LANES), LANES, 2 * rows), F32)],
        compiler_params=_params(2),
        name=name,
    )(*extra_inputs, by_seq(q), by_seq(k), by_seq(v)).reshape(n, width)


def _pad_rows(x, n):
    return jnp.concatenate([x, jnp.zeros((n - x.shape[0], x.shape[1]), x.dtype)], axis=0)


def _stacked_positions(rows, n_keys):
    t = _mod_pow2(lax.broadcasted_iota(jnp.int32, (2 * rows, n_keys), 0), rows)
    s = lax.broadcasted_iota(jnp.int32, (2 * rows, n_keys), 1)
    return t, s


def _diff_decode_body(lq1, lk1, lq2, lk2, g_ref, q_ref, kn_ref, vn_ref, ckt_ref, cv_ref, o_ref,
                      *, rows, half):
    past = ckt_ref.shape[1]
    lam = _lambda_value(lq1, lk1, lq2, lk2)
    t, s = _stacked_positions(rows, LANES)
    new_mask = (s < rows) & (_div_pow2(past + s, CHUNK) <= _div_pow2(past + t, CHUNK))
    groups = q_ref.shape[1] // LANES
    scores, weights = {}, {}

    def score(g):
        qs = _stack_halves(q_ref[:, _lanes(g)], half)
        scores[g] = (_dot(qs, ckt_ref[_lanes(g), :].astype(BF16)),
                     _dot_nt(qs, _pad_rows(kn_ref[:, _lanes(g)], LANES)))

    def weigh(g):
        s_c, s_n = scores.pop(g)
        s_n = jnp.where(new_mask, s_n, NEG_INF)
        m = jnp.maximum(jnp.max(s_c, axis=1, keepdims=True), jnp.max(s_n, axis=1, keepdims=True))
        p_c = _exp(s_c - m)
        p_n = _exp(s_n - m)
        l = jnp.sum(p_c, axis=1, keepdims=True) + jnp.sum(p_n, axis=1, keepdims=True)
        weights[g] = (p_c.astype(BF16), p_n.astype(BF16), l)

    def update(g):
        p_c, p_n, l = weights.pop(g)
        v_c = cv_ref[pl.ds(g, past, stride=groups), :]
        o = (_dot(p_c, v_c.astype(BF16)) + _dot(p_n, _pad_rows(vn_ref[:, _lanes(g)], LANES))) / l
        o = o[:rows] - lam * o[rows:]
        o_ref[:, _lanes(g)] = (_rms(o, g_ref[...]) * (1.0 - LAMBDA_INIT_LAYER0)).astype(BF16)

    _staggered([score, weigh, update], groups)


def _stick_decode_body(q_ref, kn_ref, vn_ref, ckt_ref, cvt_ref, o_ref, *, rows, half):
    past = ckt_ref.shape[1]
    suffix = _suffix_matrix(DECODE_KEYS, keys_on_rows=False)
    suffix_new = _suffix_matrix(LANES, keys_on_rows=False)
    t, s = _stacked_positions(rows, LANES)
    new_mask = (s < rows) & (s < t)
    lane = lax.broadcasted_iota(jnp.int32, (rows, LANES), 1)
    blocks = [slice(b * DECODE_KEYS, (b + 1) * DECODE_KEYS) for b in range(past // DECODE_KEYS)]
    logits, parts, laters, weights = {}, {}, {}, {}

    def logit(g):
        qs = _stack_halves(q_ref[:, _lanes(g)], half)
        logits[g] = (_dot_nt(qs, _pad_rows(kn_ref[:, _lanes(g)], LANES)),
                     _dot(qs, ckt_ref[_lanes(g), :].astype(BF16)))

    def split(g):
        z_n, z_c = logits.pop(g)
        sp_n, log_sig_n = _softplus_parts(z_n)
        sp_n = jnp.where(new_mask, sp_n, 0.0)
        sp_c, log_sig_c = _softplus_parts(z_c)
        carries = [jnp.sum(sp_n, axis=1, keepdims=True)]
        for cols in reversed(blocks[1:]):
            carries.append(carries[-1] + jnp.sum(sp_c[:, cols], axis=1, keepdims=True))
        parts[g] = (_split_bf16(sp_n), _split_bf16(sp_c))
        laters[g] = (log_sig_n, log_sig_c, carries[::-1])

    def later(g):
        (hi_n, lo_n), (hi_c, lo_c) = parts.pop(g)
        laters[g] += (_dot(hi_n, suffix_new) + _dot(lo_n, suffix_new),
                      [_dot(hi_c[:, cols], suffix) + _dot(lo_c[:, cols], suffix) for cols in blocks])

    def weigh(g):
        log_sig_n, log_sig_c, carries, later_n, later_c = laters.pop(g)
        a_n = jnp.where(new_mask, _exp(log_sig_n - later_n), 0.0).astype(BF16)
        weights[g] = (a_n, [_exp(log_sig_c[:, cols] - later_c[b] - carries[b]).astype(BF16)
                            for b, cols in enumerate(blocks)])

    def update(g):
        a_n, a_c = weights.pop(g)
        vt = cvt_ref[_lanes(g), :].astype(BF16)
        acc = _dot(a_n, _pad_rows(vn_ref[:, _lanes(g)], LANES))
        for b, cols in enumerate(blocks):
            acc = acc + _dot_nt(a_c[b], vt[:, cols])
        o_ref[:, _lanes(g)] = jnp.where(lane < half, acc[:rows], acc[rows:]).astype(BF16)

    _staggered([logit, split, later, weigh, update], q_ref.shape[1] // LANES)


def _decode_body(*refs, n_diff, rows, half):
    diff_in, stick_in, (oa_ref, ob_ref) = refs[:n_diff], refs[n_diff:-2], refs[-2:]
    _diff_decode_body(*diff_in, oa_ref, rows=rows, half=half)
    _stick_decode_body(*stick_in, ob_ref, rows=rows, half=half)


def _decode_attention(extra_inputs, extra_specs, diff_operands, stick_operands, *, rows, half):
    n, width = diff_operands[0].shape
    new_spec = pl.BlockSpec((rows, width), lambda b: (b, 0))
    cache_spec = lambda c: pl.BlockSpec((None,) + c.shape[1:], lambda b: (b,) + (0,) * (c.ndim - 1))
    operand_specs = lambda ops: [new_spec] * 3 + [cache_spec(c) for c in ops[3:]]
    out = jax.ShapeDtypeStruct((n, width), BF16)
    return pl.pallas_call(
        functools.partial(_decode_body, n_diff=len(extra_inputs) + len(diff_operands), rows=rows, half=half),
        grid=(n // rows,),
        in_specs=extra_specs + operand_specs(diff_operands) + operand_specs(stick_operands),
        out_specs=[new_spec, new_spec],
        out_shape=[out, out],
        compiler_params=_params(1),
        name="decode_attention",
    )(*extra_inputs, *diff_operands, *stick_operands)


def _merge_ffn_body(*refs, d_ff, seq_rows):
    (x_ref, oa_ref, ob_ref, ga_ref, gb_ref, wa_ref, wb_ref, wo_ref, fg_ref, wup_ref, cw_ref, cb_ref,
     wdn_ref, ng_ref) = refs[:14]
    if seq_rows is None:
        y_ref, conv_ref, acc_ref, carry_ref = refs[14:]
    else:
        h0_ref, h1_ref, y_ref, last2_ref, last1_ref, acc_ref, u_ref = refs[14:]
    d_model = x_ref.shape[1]
    parts = acc_ref.shape[0]
    tm = x_ref.shape[0] // parts
    block = lambda p: slice(p * tm, (p + 1) * tm)
    assert cw_ref.shape[0] == 3

    if seq_rows is None:
        n_slabs = tm // 8
        interleave = lambda a: jnp.swapaxes(a.reshape(8, n_slabs, d_model), 0, 1).reshape(tm, d_model)
        position_order = lambda a: jnp.swapaxes(a.reshape(n_slabs, 8, d_model), 0, 1).reshape(tm, d_model)

        @pl.when(pl.program_id(1) == 0)
        def _():
            carry_ref[...] = jnp.zeros_like(carry_ref)
        sublane = lax.broadcasted_iota(jnp.int32, (8, FFN_COLS), 0)
    else:
        interleave = position_order = lambda a: a
        pos = _mod_pow2(lax.broadcasted_iota(jnp.int32, (tm, FFN_COLS), 0), seq_rows)

    branches, mixes, projected, x1, xn = {}, {}, {}, {}, {}

    def branch(p):
        branches[p] = _dot(oa_ref[block(p), :], wa_ref[...]), _dot(ob_ref[block(p), :], wb_ref[...])

    def mixed(p):
        bra, brb = branches.pop(p)
        mixes[p] = (ga_ref[block(p), :].astype(F32) * bra + gb_ref[block(p), :].astype(F32) * brb).astype(BF16)

    def project(p):
        projected[p] = _dot(mixes.pop(p), wo_ref[...])

    def residual(p):
        x1[p] = interleave(x_ref[block(p), :] + projected.pop(p))
        xn[p] = _rms(x1[p], fg_ref[...]).astype(BF16)

    def finish(p):
        y_ref[block(p), :] = position_order(_rms(x1[p] + acc_ref[p], ng_ref[...]))

    def taps(u, u1, u2, cols):
        return cb_ref[:, cols] + cw_ref[0:1, cols] * u2 + cw_ref[1:2, cols] * u1 + cw_ref[2:3, cols] * u

    def conv(u, cols):
        if seq_rows is None:
            prev2 = carry_ref[7:8, cols]
            prev1 = carry_ref[15:16, cols]
            carry_ref[:, cols] = u[tm - 16:, :]
            conv_ref[:, cols] = u[tm - 16:, :]
            before1 = jnp.where(sublane == 0, prev1, pltpu.roll(u[tm - 8:, :], 1, 0))
            before2 = jnp.where(sublane == 0, prev2, pltpu.roll(u[tm - 16:tm - 8, :], 1, 0))
            u1 = jnp.concatenate([before1, u[:tm - 8, :]], axis=0)
            u2 = jnp.concatenate([before2, before1, u[:tm - 16, :]], axis=0)
            return taps(u, u1, u2, cols)
        else:
            n_seq = tm // seq_rows
            expand = lambda h: jnp.broadcast_to(h[:, None, :], (n_seq, seq_rows, FFN_COLS)).reshape(tm, FFN_COLS)
            prev2 = expand(h0_ref[:, cols])
            prev1 = expand(h1_ref[:, cols])
            for part in range(FFN_COLS // LANES):
                lanes = _lanes(part)
                out = slice(cols.start + part * LANES, cols.start + (part + 1) * LANES)
                u_ref[...] = u[:, lanes]
                last2_ref[:, out] = u_ref[pl.ds(seq_rows - 2, n_seq, stride=seq_rows), :]
                last1_ref[:, out] = u_ref[pl.ds(seq_rows - 1, n_seq, stride=seq_rows), :]
            u1 = jnp.where(pos == 0, prev1, pltpu.roll(u, 1, 0))
            u2 = jnp.where(pos == 0, prev2, jnp.where(pos == 1, prev1, pltpu.roll(u, 2, 0)))
            return taps(u, u1, u2, cols)

    chunks = d_ff // FFN_COLS
    gate_cols = lambda c: slice(c * FFN_COLS, (c + 1) * FFN_COLS)
    val_cols = lambda c: slice(d_ff + c * FFN_COLS, d_ff + (c + 1) * FFN_COLS)
    ups, hidden = {}, {}

    def up(i):
        p, c = divmod(i, chunks)
        ups[i] = _dot(xn[p], wup_ref[:, gate_cols(c)]), _dot(xn[p], wup_ref[:, val_cols(c)])

    def gated(i):
        c = i % chunks
        ug, uv = ups.pop(i)
        gate = conv(ug, gate_cols(c))
        hidden[i] = (gate * conv(uv, val_cols(c)) / (1.0 + _exp(gate, -1.0))).astype(BF16)

    def down(i):
        p, c = divmod(i, chunks)
        part = _dot(hidden.pop(i), wdn_ref[gate_cols(c), :])
        acc_ref[p] = part if c == 0 else acc_ref[p] + part

    merge = [branch, mixed, project, residual]
    for step in merge:
        step(0)
    for t in range(parts * chunks + 2):
        for stage, fn in ((0, up), (2, down), (1, gated)):
            if 0 <= t - stage < parts * chunks:
                fn(t - stage)
        p, c = divmod(t, chunks)
        if p + 1 < parts and 1 <= c <= len(merge):
            merge[c - 1](p + 1)
        if t >= 2 and (t - 2) % chunks == chunks - 1:
            finish((t - 2) // chunks)


def _merge_ffn(x2d, oa, ob, ga, gb, wa, wb, wo, fg, wup, cw, cb, wdn, ng, *, batch, history=None):
    n, d_model = x2d.shape
    width = oa.shape[1]
    d_ff = wdn.shape[0]
    two_ff = wup.shape[1]
    seq = n // batch
    consts = [wa, wb, wo, fg, wup, cw, cb, wdn, ng]
    const_specs = [_const_spec(a.shape) for a in consts]
    if history is None:
        tm = min(FFN_ROWS, seq)
        nt = seq // tm
        grid = (batch, nt)
        row = lambda b, j: (b * nt + j, 0)
        extra, extra_specs = [], []
        conv_shape = [jax.ShapeDtypeStruct((batch, 16, two_ff), F32)]
        conv_spec = [pl.BlockSpec((None, 16, two_ff), lambda b, j: (b, 0, 0))]
        scratch = [pltpu.VMEM((FFN_BLOCKS, tm // FFN_BLOCKS, d_model), F32), pltpu.VMEM((16, two_ff), F32)]
        seq_rows = None
    else:
        tm = n
        grid = (1,)
        row = lambda i: (0, 0)
        extra = list(history)
        extra_specs = [_const_spec(h.shape) for h in history]
        conv_shape = [jax.ShapeDtypeStruct((batch, two_ff), F32)] * 2
        conv_spec = [pl.BlockSpec((batch, two_ff), row)] * 2
        scratch = [pltpu.VMEM((1, tm, d_model), F32), pltpu.VMEM((tm, LANES), F32)]
        seq_rows = seq
    tile = lambda cols: pl.BlockSpec((tm, cols), row)
    return pl.pallas_call(
        functools.partial(_merge_ffn_body, d_ff=d_ff, seq_rows=seq_rows),
        grid=grid,
        in_specs=[tile(d_model), tile(width), tile(width), tile(d_model), tile(d_model)]
                 + const_specs + extra_specs,
        out_specs=[tile(d_model)] + conv_spec,
        out_shape=[jax.ShapeDtypeStruct((n, d_model), F32)] + conv_shape,
        scratch_shapes=scratch,
        compiler_params=_params(len(grid)),
        name="merge_ffn" if history is None else "merge_ffn_streams",
    )(x2d, oa, ob, ga, gb, *consts, *extra)


def kernel(x_prompt, x_sample, cache_diff_k, cache_diff_v, cache_sb_k, cache_sb_v, state_conv,
           attn_norm_g, w_in, lambda_q1, lambda_k1, lambda_q2, lambda_k2, subln_g, w_branch_a,
           w_branch_b, w_out, ffn_norm_g, w_up, conv_w, conv_b, w_down, final_norm_g):
    depth = w_in.shape[0]
    assert depth == 1, "single-layer trunk only"
    batch, seq, d_model = x_prompt.shape
    dec_batch, dec_seq, _ = x_sample.shape
    past = cache_diff_k.shape[2]
    width = d_model // 2
    da_heads, _, da_dim = cache_diff_k.shape[3:]
    da_v = cache_diff_v.shape[-1]
    sb_heads, sb_dim = cache_sb_k.shape[3:]
    assert da_dim == sb_dim and 2 * da_dim == LANES and da_v == LANES
    half = da_dim

    bf = lambda w: w[0].astype(BF16)
    row1 = lambda v: v.reshape(1, -1)
    w_in_bf = bf(w_in)
    consts = (bf(w_branch_a), bf(w_branch_b), bf(w_out), ffn_norm_g, bf(w_up), conv_w[0], conv_b,
              bf(w_down), row1(final_norm_g))
    lams = [row1(v[0]) for v in (lambda_q1, lambda_k1, lambda_q2, lambda_k2)]
    g_row, g_col = row1(subln_g[0]), subln_g[0].reshape(-1, 1)
    specs = lambda arrays: [_const_spec(a.shape) for a in arrays]

    xp = x_prompt.reshape(batch * seq, d_model)
    cos_p, sin_p = _rope_tables(jnp.arange(seq), da_dim)
    (p_dk, p_dv, p_sk, p_sv, qd, kd, vd, qs, ks, vs, ga, gb) = _in_proj(
        xp, attn_norm_g, w_in_bf, cos_p, sin_p, head_dim=da_dim, seq=seq)
    major = lambda c, *dims: jnp.moveaxis(c.reshape(batch, *dims, seq), -1, 1)
    p_dk, p_sk, p_sv = (major(p_dk, da_heads, 2, da_dim), major(p_sk, sb_heads, sb_dim),
                        major(p_sv, sb_heads, sb_dim))
    oa = _prompt_attention(_diff_prompt_body, "diff_prompt", lams + [g_col], specs(lams + [g_col]), qd, kd, vd,
                           batch=batch, seq=seq, rows=DIFF_ROWS, half=half)
    ob = _prompt_attention(_stick_prompt_body, "stick_prompt", [], [], qs, ks, vs,
                           batch=batch, seq=seq, rows=STICK_ROWS, half=half)
    y_p, conv_p = _merge_ffn(xp, oa, ob, ga, gb, *consts, batch=batch)

    xs = x_sample.reshape(dec_batch * dec_seq, d_model)
    cos_s, sin_s = _rope_tables(past + jnp.arange(dec_seq), da_dim)
    cos_s, sin_s = jnp.tile(cos_s, (dec_batch, 1)), jnp.tile(sin_s, (dec_batch, 1))
    (s_dk, s_dv, s_sk, s_sv, qd, kd, vd, qs, ks, vs, ga, gb) = _in_proj(
        xs, attn_norm_g, w_in_bf, cos_s, sin_s, head_dim=da_dim)
    minor = lambda c: jnp.moveaxis(c[0], 1, -1).reshape(dec_batch, width, past)
    oa, ob = _decode_attention(
        lams + [g_row], specs(lams + [g_row]),
        [qd, kd, vd, minor(cache_diff_k), cache_diff_v[0].reshape(dec_batch, past * da_heads, da_v)],
        [qs, ks, vs, minor(cache_sb_k), minor(cache_sb_v)], rows=dec_seq, half=half)
    y_s, last2, last1 = _merge_ffn(xs, oa, ob, ga, gb, *consts, batch=dec_batch,
                                   history=(state_conv[0, :, 0], state_conv[0, :, 1]))

    conv_p = jnp.stack([conv_p[:, 7], conv_p[:, 15]], axis=1)
    conv_s = jnp.stack([last2, last1], axis=1)
    return (y_p.reshape(batch, seq, d_model), y_s.reshape(dec_batch, dec_seq, d_model),
            p_dk[None], p_dv.reshape(depth, batch, seq, da_heads, da_v), p_sk[None], p_sv[None],
            conv_p[None],
            s_dk.reshape(depth, dec_batch, dec_seq, da_heads, 2, da_dim),
            s_dv.reshape(depth, dec_batch, dec_seq, da_heads, da_v),
            s_sk.reshape(depth, dec_batch, dec_seq, sb_heads, sb_dim),
            s_sv.reshape(depth, dec_batch, dec_seq, sb_heads, sb_dim),
            conv_s[None])
```
